```python
import jax, jax.numpy as jnp
from jax import lax
import numpy as np

D_MODEL = 1024
BATCH = 8
SEQ = 4096
DEPTH = 4

N_EVEN = (DEPTH + 1) // 2
N_ODD = DEPTH // 2

SB_HEADS = 8
SB_HEAD_DIM = 64
SB_WIDTH = SB_HEADS * SB_HEAD_DIM
Q_BLOCK = 128
CONV_WIDTH = D_MODEL // 2
CONV_KERNEL = 31
EVEN_IN = 3 * SB_WIDTH + 2 * CONV_WIDTH
EVEN_MIX = SB_WIDTH + CONV_WIDTH
POOL_WINDOWS = (2, 4, 8, 16)
POOL_GROUP = D_MODEL // len(POOL_WINDOWS)
N_EXPERTS = 64
TOP_K = 8
N_GROUPS = 8
TOPK_GROUPS = 4
EXPERT_FF = 256
SHARED_FF = 256
ROUTED_SCALE = 2.5
MOE_BLOCK = 256
EPS = 1e-6

kernel_name = "hybrid_sb_conformer_pool_moe_adaln"


def rmsnorm(x, g):
    xf = x.astype(jnp.float32)
    y = xf * lax.rsqrt(jnp.mean(xf * xf, axis=-1, keepdims=True) + EPS)
    return (y * g.astype(jnp.float32)).astype(x.dtype)


def layernorm(x, g, b):
    xf = x.astype(jnp.float32)
    mu = jnp.mean(xf, axis=-1, keepdims=True)
    var = jnp.mean(jnp.square(xf - mu), axis=-1, keepdims=True)
    y = (xf - mu) * lax.rsqrt(var + EPS)
    return (y * g.astype(jnp.float32) + b.astype(jnp.float32)).astype(x.dtype)


def modulate(h, shift, scale):
    return h * (1 + scale) + shift


def stick_breaking_attention(q, k, v):
    S = q.shape[1]
    scale = SB_HEAD_DIM ** -0.5
    outs = []
    for i in range(S // Q_BLOCK):
        t0, t1 = i * Q_BLOCK, (i + 1) * Q_BLOCK
        qb, kb, vb = q[:, t0:t1], k[:, :t1], v[:, :t1]
        z = jnp.einsum('bqhd,bkhd->bhqk', qb, kb).astype(jnp.float32) * scale
        causal = jnp.arange(t1)[None, :] < jnp.arange(t0, t1)[:, None]
        log_beta = jax.nn.log_sigmoid(z)
        log_1m = jnp.where(causal, jax.nn.log_sigmoid(-z), 0.0)
        log_rem = lax.cumsum(log_1m, axis=3, reverse=True) - log_1m
        a = jnp.where(causal, jnp.exp(log_beta + log_rem), 0.0)
        outs.append(jnp.einsum('bhqk,bkhd->bqhd', a.astype(vb.dtype), vb))
    return jnp.concatenate(outs, axis=1)


def causal_depthwise_conv(u, w, b):
    y = lax.conv_general_dilated(u, w, window_strides=(1,), padding=[(CONV_KERNEL - 1, 0)],
                                 dimension_numbers=('NWC', 'WIO', 'NWC'),
                                 feature_group_count=u.shape[-1])
    return y + b


def even_mixer(h, w_in, dw_w, dw_b, ln_g, ln_b, w_out):
    B, S, _ = h.shape
    proj = h @ w_in
    q = proj[..., 0:SB_WIDTH].reshape(B, S, SB_HEADS, SB_HEAD_DIM)
    k = proj[..., SB_WIDTH:2 * SB_WIDTH].reshape(B, S, SB_HEADS, SB_HEAD_DIM)
    v = proj[..., 2 * SB_WIDTH:3 * SB_WIDTH].reshape(B, S, SB_HEADS, SB_HEAD_DIM)
    a_val = proj[..., 3 * SB_WIDTH:3 * SB_WIDTH + CONV_WIDTH]
    a_gate = proj[..., 3 * SB_WIDTH + CONV_WIDTH:]
    att = stick_breaking_attention(q, k, v).reshape(B, S, SB_WIDTH)
    u = a_val * jax.nn.sigmoid(a_gate)
    u = causal_depthwise_conv(u, dw_w, dw_b)
    u = jax.nn.silu(layernorm(u, ln_g, ln_b))
    return jnp.concatenate([att, u], axis=-1) @ w_out


def trailing_mean(u, w):
    S = u.shape[1]
    cs = jnp.cumsum(u.astype(jnp.float32), axis=1)
    prev = jnp.pad(cs, ((0, 0), (w, 0), (0, 0)))[:, :S]
    cnt = jnp.minimum(jnp.arange(1, S + 1), w).astype(jnp.float32)[None, :, None]
    return ((cs - prev) / cnt).astype(u.dtype)


def odd_mixer(h, w_in, w_grp, ch_scale, w_out):
    u = h @ w_in
    outs = []
    for gi, w in enumerate(POOL_WINDOWS):
        ug = u[..., gi * POOL_GROUP:(gi + 1) * POOL_GROUP]
        pooled = trailing_mean(ug, w) - ug
        outs.append(jnp.einsum('bsc,cd->bsd', pooled, w_grp[gi]))
    y = jnp.concatenate(outs, axis=-1) * ch_scale
    return y @ w_out


def route(xf, w_router, bias):
    N = xf.shape[0]
    per_group = N_EXPERTS // N_GROUPS
    scores = jax.nn.sigmoid(xf.astype(jnp.float32) @ w_router.astype(jnp.float32))
    biased = scores + bias.astype(jnp.float32)
    grp_score = lax.top_k(biased.reshape(N, N_GROUPS, per_group), 2)[0].sum(-1)
    _, gidx = lax.top_k(grp_score, TOPK_GROUPS)
    gmask = jax.nn.one_hot(gidx, N_GROUPS, dtype=jnp.float32).sum(axis=1) > 0
    emask = jnp.repeat(gmask, per_group, axis=1)
    _, idx = lax.top_k(jnp.where(emask, biased, -jnp.inf), TOP_K)
    wts = jnp.take_along_axis(scores, idx, axis=1)
    wts = wts / jnp.sum(wts, axis=-1, keepdims=True) * ROUTED_SCALE
    return idx, wts


def routed_experts(xf, idx, wts, w1, w3, w2):
    N, D = xf.shape
    M = N * TOP_K
    flat_e = idx.reshape(-1)
    flat_tok = jnp.arange(M, dtype=jnp.int32) // TOP_K
    flat_w = wts.reshape(-1)
    order = jnp.argsort(flat_e)
    e_s, tok_s, w_s = flat_e[order], flat_tok[order], flat_w[order]
    counts = jnp.bincount(flat_e, length=N_EXPERTS)
    padded = (counts + MOE_BLOCK - 1) // MOE_BLOCK * MOE_BLOCK
    pend = jnp.cumsum(padded)
    pstart = pend - padded
    gstart = jnp.cumsum(counts) - counts
    dest = pstart[e_s] + (jnp.arange(M) - gstart[e_s])
    n_blocks = (M + MOE_BLOCK - 1) // MOE_BLOCK + N_EXPERTS
    P = n_blocks * MOE_BLOCK
    slot_tok = jnp.full((P,), N, dtype=jnp.int32).at[dest].set(tok_s)
    slot_w = jnp.zeros((P,), jnp.float32).at[dest].set(w_s)
    block_e = jnp.clip(jnp.searchsorted(pend, jnp.arange(n_blocks) * MOE_BLOCK, side='right'),
                       0, N_EXPERTS - 1)
    xpad = jnp.concatenate([xf, jnp.zeros((1, D), xf.dtype)], axis=0)

    def body(acc, blk):
        tok_b, w_b, e = blk
        xb = xpad[tok_b]
        hb = jax.nn.silu(xb @ w1[e]) * (xb @ w3[e])
        yb = (hb @ w2[e]).astype(jnp.float32) * w_b[:, None]
        return acc.at[tok_b].add(yb), None

    acc, _ = lax.scan(body, jnp.zeros((N + 1, D), jnp.float32),
                      (slot_tok.reshape(n_blocks, MOE_BLOCK), slot_w.reshape(n_blocks, MOE_BLOCK), block_e))
    return acc[:N]


def moe_ffn(h, w_router, bias, w1, w3, w2, s1, s3, s2):
    B, S, D = h.shape
    xf = h.reshape(B * S, D)
    idx, wts = route(xf, w_router, bias)
    routed = routed_experts(xf, idx, wts, w1, w3, w2)
    shared = (jax.nn.silu(xf @ s1) * (xf @ s3)) @ s2
    return (shared.astype(jnp.float32) + routed).astype(h.dtype).reshape(B, S, D)


def setup_inputs(seed: int = 0) -> dict:
    key = jax.random.key(seed)
    ks = jax.random.split(key, 26)
    f32 = jnp.float32
    D = D_MODEL
    nrm = lambda k, shape, fan: jax.random.normal(k, shape, f32) * (fan ** -0.5)
    rnd = lambda k, shape, s: s * jax.random.normal(k, shape, f32)
    return {
        "x": jax.random.normal(ks[0], (BATCH, SEQ, D), f32),
        "c": jax.random.normal(ks[1], (BATCH, D), f32),
        "ada_w": rnd(ks[2], (DEPTH, D, 6 * D), 0.01),
        "ada_b": rnd(ks[3], (DEPTH, 6 * D), 0.02),
        "norm1_g": 1.0 + rnd(ks[4], (DEPTH, D), 0.05),
        "norm2_g": 1.0 + rnd(ks[5], (DEPTH, D), 0.05),
        "ev_w_in": nrm(ks[6], (N_EVEN, D, EVEN_IN), D),
        "ev_dw_w": nrm(ks[7], (N_EVEN, CONV_KERNEL, 1, CONV_WIDTH), CONV_KERNEL),
        "ev_dw_b": rnd(ks[8], (N_EVEN, CONV_WIDTH), 0.02),
        "ev_ln_g": 1.0 + rnd(ks[9], (N_EVEN, CONV_WIDTH), 0.05),
        "ev_ln_b": rnd(ks[10], (N_EVEN, CONV_WIDTH), 0.02),
        "ev_w_out": nrm(ks[11], (N_EVEN, EVEN_MIX, D), EVEN_MIX),
        "od_w_in": nrm(ks[12], (N_ODD, D, D), D),
        "od_w_grp": nrm(ks[13], (N_ODD, len(POOL_WINDOWS), POOL_GROUP, POOL_GROUP), POOL_GROUP),
        "od_scale": 1.0 + rnd(ks[14], (N_ODD, D), 0.1),
        "od_w_out": nrm(ks[15], (N_ODD, D, D), D),
        "moe_w_router": nrm(ks[16], (DEPTH, D, N_EXPERTS), D),
        "moe_bias": rnd(ks[17], (DEPTH, N_EXPERTS), 0.01),
        "moe_w1": nrm(ks[18], (DEPTH, N_EXPERTS, D, EXPERT_FF), D),
        "moe_w3": nrm(ks[19], (DEPTH, N_EXPERTS, D, EXPERT_FF), D),
        "moe_w2": nrm(ks[20], (DEPTH, N_EXPERTS, EXPERT_FF, D), EXPERT_FF),
        "sh_w1": nrm(ks[21], (DEPTH, D, SHARED_FF), D),
        "sh_w3": nrm(ks[22], (DEPTH, D, SHARED_FF), D),
        "sh_w2": nrm(ks[23], (DEPTH, SHARED_FF, D), SHARED_FF),
        "final_g": 1.0 + rnd(ks[24], (D,), 0.05),
    }


def reference(x, c, ada_w, ada_b, norm1_g, norm2_g, ev_w_in, ev_dw_w, ev_dw_b, ev_ln_g, ev_ln_b,
              ev_w_out, od_w_in, od_w_grp, od_scale, od_w_out, moe_w_router, moe_bias, moe_w1,
              moe_w3, moe_w2, sh_w1, sh_w3, sh_w2, final_g):
    cond = jax.nn.silu(c)
    for l in range(DEPTH):
        mod = cond @ ada_w[l] + ada_b[l]
        sh1, sc1, g1, sh2, sc2, g2 = [m[:, None, :] for m in jnp.split(mod, 6, axis=-1)]
        h = modulate(rmsnorm(x, norm1_g[l]), sh1, sc1)
        i = l // 2
        if l % 2 == 0:
            y = even_mixer(h, ev_w_in[i], ev_dw_w[i], ev_dw_b[i], ev_ln_g[i], ev_ln_b[i], ev_w_out[i])
        else:
            y = odd_mixer(h, od_w_in[i], od_w_grp[i], od_scale[i], od_w_out[i])
        x = x + g1 * y
        h = modulate(rmsnorm(x, norm2_g[l]), sh2, sc2)
        x = x + g2 * moe_ffn(h, moe_w_router[l], moe_bias[l], moe_w1[l], moe_w3[l], moe_w2[l],
                             sh_w1[l], sh_w3[l], sh_w2[l])
    return rmsnorm(x, final_g)
```

```python
import functools

import jax
import jax.numpy as jnp
from jax import lax
from jax.experimental import pallas as pl
from jax.experimental.pallas import tpu as pltpu

F32 = jnp.float32
BF16 = jnp.bfloat16

EPS = 1e-6
SB_HEADS = 8
SB_HEAD_DIM = 64
SB_WIDTH = SB_HEADS * SB_HEAD_DIM
POOL_WINDOWS = (2, 4, 8, 16)
N_GROUPS = 8
TOPK_GROUPS = 4
TOP_K = 8
ROUTED_SCALE = 2.5

VMEM_LIMIT_BYTES = 56 * 1024 * 1024
LANES = 128


def _params(*semantics):
    return pltpu.CompilerParams(dimension_semantics=semantics, vmem_limit_bytes=VMEM_LIMIT_BYTES)


def _sigmoid(x):
    return 1.0 / (1.0 + jnp.exp(-x))


def _norm_mod(x, g, shift, scale):
    ms = jnp.mean(x * x, axis=-1, keepdims=True)
    y = x * lax.rsqrt(ms + EPS) * g
    return y * (1.0 + scale) + shift


def _ada_kernel(c_ref, w_ref, b_ref, o_ref):
    c = c_ref[...]
    cond = c * _sigmoid(c)
    o_ref[0] = jnp.dot(cond, w_ref[0], precision=lax.Precision.HIGHEST,
                       preferred_element_type=F32) + b_ref[0]


def _ada(c, ada_w, ada_b):
    depth, d, n6 = ada_w.shape
    b = c.shape[0]
    tn = 1536
    return pl.pallas_call(
        _ada_kernel,
        grid=(depth, n6 // tn),
        in_specs=[
            pl.BlockSpec((b, d), lambda l, j: (0, 0)),
            pl.BlockSpec((1, d, tn), lambda l, j: (l, 0, j)),
            pl.BlockSpec((1, 1, tn), lambda l, j: (l, 0, j)),
        ],
        out_specs=pl.BlockSpec((1, b, tn), lambda l, j: (l, 0, j)),
        out_shape=jax.ShapeDtypeStruct((depth, b, n6), F32),
        compiler_params=_params("parallel", "parallel"),
        name="ada_mod",
    )(c, ada_w, ada_b.reshape(depth, 1, n6))


def _even_in_kernel(x_ref, g_ref, sh_ref, sc_ref, w_ref, qkv_ref, u_ref):
    h = _norm_mod(x_ref[0], g_ref[...], sh_ref[0], sc_ref[0]).astype(BF16)
    n_qkv = qkv_ref.shape[-1]
    cw = u_ref.shape[-1]
    for j in range(n_qkv // cw):
        qkv_ref[0, :, j * cw:(j + 1) * cw] = jnp.dot(
            h, w_ref[:, j * cw:(j + 1) * cw], preferred_element_type=F32).astype(BF16)
    val = jnp.dot(h, w_ref[:, n_qkv:n_qkv + cw], preferred_element_type=F32)
    gate = jnp.dot(h, w_ref[:, n_qkv + cw:], preferred_element_type=F32)
    u_ref[0] = val * _sigmoid(gate)


def _even_in(x, g, shift, scale, w_in, tm):
    b, s, d = x.shape
    n_qkv = 3 * SB_WIDTH
    cw = (w_in.shape[1] - n_qkv) // 2
    return pl.pallas_call(
        _even_in_kernel,
        grid=(b, s // tm),
        in_specs=[
            pl.BlockSpec((1, tm, d), lambda bi, i: (bi, i, 0)),
            pl.BlockSpec((1, d), lambda bi, i: (0, 0)),
            pl.BlockSpec((1, 1, d), lambda bi, i: (bi, 0, 0)),
            pl.BlockSpec((1, 1, d), lambda bi, i: (bi, 0, 0)),
            pl.BlockSpec(w_in.shape, lambda bi, i: (0, 0)),
        ],
        out_specs=[
            pl.BlockSpec((1, tm, n_qkv), lambda bi, i: (bi, i, 0)),
            pl.BlockSpec((1, tm, cw), lambda bi, i: (bi, i, 0)),
        ],
        out_shape=[
            jax.ShapeDtypeStruct((b, s, n_qkv), BF16),
            jax.ShapeDtypeStruct((b, s, cw), F32),
        ],
        compiler_params=_params("parallel", "parallel"),
        name="even_in",
    )(x, g.reshape(1, d), shift, scale, w_in)


def _sb_kernel(qT_ref, k_ref, vT_ref, tri_ref, o_ref, *, tq, tk):
    qi = pl.program_id(2)
    qT = qT_ref[0, 0]
    tri2 = tri_ref[...]
    n_diag = tq // tk
    n_off = qi * n_diag
    dv = o_ref.shape[2]

    def block(j, acc, r_sum, valid):
        zT = jnp.dot(k_ref[0, 0, j], qT, preferred_element_type=F32)
        sp = jnp.maximum(zT, 0.0) + jnp.log(1.0 + jnp.exp(-jnp.abs(zT)))
        if valid is not None:
            sp = jnp.where(valid, sp, 0.0)
        hi = sp.astype(BF16)
        lo = (sp - hi.astype(F32)).astype(BF16)
        csum = jnp.dot(tri2, jnp.concatenate([hi, lo], axis=0), preferred_element_type=F32)
        a = jnp.exp(zT - csum - r_sum)
        if valid is not None:
            a = jnp.where(valid, a, 0.0)
        acc = acc + jnp.dot(vT_ref[0, 0, j], a.astype(BF16), preferred_element_type=F32)
        return acc, r_sum + csum[0:1, :]

    acc = jnp.zeros((dv, tq), F32)
    r_sum = jnp.zeros((1, tq), F32)
    row = lax.broadcasted_iota(jnp.int32, (tk, tq), 0)
    col = lax.broadcasted_iota(jnp.int32, (tk, tq), 1)
    for jd in range(n_diag - 1, -1, -1):
        acc, r_sum = block(n_off + jd, acc, r_sum, (row + jd * tk) < col)

    def body(i, carry):
        return block(n_off - 1 - i, carry[0], carry[1], None)

    acc, r_sum = lax.fori_loop(0, n_off, body, (acc, r_sum))
    o_ref[0, 0] = acc.astype(o_ref.dtype)


def _sb_attention(qkv, tq, tk):
    b, s, _ = qkv.shape
    h, dh = SB_HEADS, SB_HEAD_DIM
    nkb = s // tk
    q = qkv[..., :SB_WIDTH].reshape(b, s, h, dh) * (dh ** -0.5)
    k = qkv[..., SB_WIDTH:2 * SB_WIDTH].reshape(b, s, h, dh)
    v = qkv[..., 2 * SB_WIDTH:].reshape(b, s, h, dh)
    pad = LANES - dh
    qT = jnp.pad(q.transpose(0, 2, 3, 1), ((0, 0), (0, 0), (0, pad), (0, 0)))
    kb = jnp.pad(k.transpose(0, 2, 1, 3), ((0, 0), (0, 0), (0, 0), (0, pad)))
    kb = kb.reshape(b, h, nkb, tk, LANES)
    vT = v.transpose(0, 2, 3, 1).reshape(b, h, dh, nkb, tk).transpose(0, 1, 3, 2, 4)
    u = (jnp.arange(tk)[None, :] >= jnp.arange(tk)[:, None]).astype(BF16)
    tri2 = jnp.concatenate([u, u], axis=1)
    oT = pl.pallas_call(
        functools.partial(_sb_kernel, tq=tq, tk=tk),
        grid=(b, h, s // tq),
        in_specs=[
            pl.BlockSpec((1, 1, LANES, tq), lambda bi, hi, qi: (bi, hi, 0, qi)),
            pl.BlockSpec((1, 1, nkb, tk, LANES), lambda bi, hi, qi: (bi, hi, 0, 0, 0)),
            pl.BlockSpec((1, 1, nkb, dh, tk), lambda bi, hi, qi: (bi, hi, 0, 0, 0)),
            pl.BlockSpec((tk, 2 * tk), lambda bi, hi, qi: (0, 0)),
        ],
        out_specs=pl.BlockSpec((1, 1, dh, tq), lambda bi, hi, qi: (bi, hi, 0, qi)),
        out_shape=jax.ShapeDtypeStruct((b, h, dh, s), BF16),
        compiler_params=_params("parallel", "parallel", "arbitrary"),
        name="sb_attention",
    )(qT, kb, vT, tri2)
    return oT.transpose(0, 3, 1, 2).reshape(b, s, SB_WIDTH)


CONV_HALO = 32


def _even_post_kernel(x_ref, att_ref, ucur_ref, uhalo_ref, dww_ref, dwb_ref, lng_ref, lnb_ref,
                      wout_ref, g1_ref, o_ref, ext_ref, *, tm, taps):
    i = pl.program_id(1)
    ext_ref[0:CONV_HALO, :] = jnp.where(i > 0, uhalo_ref[0], 0.0)
    ext_ref[CONV_HALO:CONV_HALO + tm, :] = ucur_ref[0]
    base = CONV_HALO - (taps - 1)
    acc = ext_ref[base:base + tm, :] * dww_ref[0:1, :]
    for k in range(1, taps):
        acc = acc + ext_ref[base + k:base + k + tm, :] * dww_ref[k:k + 1, :]
    conv = acc + dwb_ref[...]
    mu = jnp.mean(conv, axis=-1, keepdims=True)
    cen = conv - mu
    var = jnp.mean(cen * cen, axis=-1, keepdims=True)
    y = cen * lax.rsqrt(var + EPS) * lng_ref[...] + lnb_ref[...]
    c = (y * _sigmoid(y)).astype(BF16)
    cat = jnp.concatenate([att_ref[0], c], axis=-1)
    out = jnp.dot(cat, wout_ref[...], preferred_element_type=F32)
    o_ref[0] = x_ref[0] + g1_ref[0] * out


def _even_post(x, att, u, dw_w, dw_b, ln_g, ln_b, w_out, g1, tm):
    b, s, d = x.shape
    cw = u.shape[-1]
    taps = dw_w.shape[0]
    hb = tm // CONV_HALO
    return pl.pallas_call(
        functools.partial(_even_post_kernel, tm=tm, taps=taps),
        grid=(b, s // tm),
        in_specs=[
            pl.BlockSpec((1, tm, d), lambda bi, i: (bi, i, 0)),
            pl.BlockSpec((1, tm, att.shape[-1]), lambda bi, i: (bi, i, 0)),
            pl.BlockSpec((1, tm, cw), lambda bi, i: (bi, i, 0)),
            pl.BlockSpec((1, CONV_HALO, cw), lambda bi, i: (bi, jnp.maximum(i * hb - 1, 0), 0)),
            pl.BlockSpec((taps, cw), lambda bi, i: (0, 0)),
            pl.BlockSpec((1, cw), lambda bi, i: (0, 0)),
            pl.BlockSpec((1, cw), lambda bi, i: (0, 0)),
            pl.BlockSpec((1, cw), lambda bi, i: (0, 0)),
            pl.BlockSpec(w_out.shape, lambda bi, i: (0, 0)),
            pl.BlockSpec((1, 1, d), lambda bi, i: (bi, 0, 0)),
        ],
        out_specs=pl.BlockSpec((1, tm, d), lambda bi, i: (bi, i, 0)),
        out_shape=jax.ShapeDtypeStruct((b, s, d), F32),
        scratch_shapes=[pltpu.VMEM((CONV_HALO + tm, cw), F32)],
        compiler_params=_params("parallel", "parallel"),
        name="even_post",
    )(x, att, u, u, dw_w.reshape(taps, cw), dw_b.reshape(1, cw), ln_g.reshape(1, cw),
      ln_b.reshape(1, cw), w_out, g1)


def _odd_in_kernel(x_ref, g_ref, sh_ref, sc_ref, w_ref, u_ref):
    h = _norm_mod(x_ref[0], g_ref[...], sh_ref[0], sc_ref[0]).astype(BF16)
    u_ref[0] = jnp.dot(h, w_ref[...], preferred_element_type=F32)


def _odd_in(x, g, shift, scale, w_in, tm):
    b, s, d = x.shape
    return pl.pallas_call(
        _odd_in_kernel,
        grid=(b, s // tm),
        in_specs=[
            pl.BlockSpec((1, tm, d), lambda bi, i: (bi, i, 0)),
            pl.BlockSpec((1, d), lambda bi, i: (0, 0)),
            pl.BlockSpec((1, 1, d), lambda bi, i: (bi, 0, 0)),
            pl.BlockSpec((1, 1, d), lambda bi, i: (bi, 0, 0)),
            pl.BlockSpec(w_in.shape, lambda bi, i: (0, 0)),
        ],
        out_specs=pl.BlockSpec((1, tm, w_in.shape[1]), lambda bi, i: (bi, i, 0)),
        out_shape=jax.ShapeDtypeStruct((b, s, w_in.shape[1]), F32),
        compiler_params=_params("parallel", "parallel"),
        name="odd_in",
    )(x, g.reshape(1, d), shift, scale, w_in)


POOL_HALO = 16
POOL_PAD = 8


def _odd_post_kernel(x_ref, ucur_ref, uhalo_ref, wgrp_ref, scale_ref, wout_ref, g1_ref, o_ref,
                     e_ref, s1_ref, s2_ref, s4_ref, *, tm):
    i = pl.program_id(1)
    d = e_ref.shape[1]
    gw = d // len(POOL_WINDOWS)
    ext = POOL_HALO + tm
    lo = POOL_PAD
    zeros_pad = jnp.zeros((POOL_PAD, d), F32)
    e_ref[0:lo, :] = zeros_pad
    e_ref[lo:lo + POOL_HALO, :] = jnp.where(i > 0, uhalo_ref[0], 0.0)
    e_ref[lo + POOL_HALO:lo + ext, :] = ucur_ref[0]
    s1_ref[0:lo, :] = zeros_pad
    s1_ref[lo:lo + ext, :] = e_ref[lo:lo + ext, :] + e_ref[lo - 1:lo - 1 + ext, :]
    s2_ref[0:lo, :] = zeros_pad
    s2_ref[lo:lo + ext, :] = s1_ref[lo:lo + ext, :] + s1_ref[lo - 2:lo - 2 + ext, :]
    s4_ref[0:lo, :] = zeros_pad
    s4_ref[lo:lo + ext, :] = s2_ref[lo:lo + ext, :] + s2_ref[lo - 4:lo - 4 + ext, :]
    t0 = lo + POOL_HALO
    pos = i * tm + lax.broadcasted_iota(jnp.int32, (tm, 1), 0) + 1
    sums = (
        s1_ref[t0:t0 + tm, 0:gw],
        s2_ref[t0:t0 + tm, gw:2 * gw],
        s4_ref[t0:t0 + tm, 2 * gw:3 * gw],
        s4_ref[t0:t0 + tm, 3 * gw:4 * gw] + s4_ref[t0 - 8:t0 - 8 + tm, 3 * gw:4 * gw],
    )
    outs = []
    for gi, w in enumerate(POOL_WINDOWS):
        cnt = jnp.minimum(pos, w).astype(F32)
        ug = e_ref[t0:t0 + tm, gi * gw:(gi + 1) * gw]
        pooled = sums[gi] / cnt - ug
        outs.append(jnp.dot(pooled.astype(BF16), wgrp_ref[gi], preferred_element_type=F32))
    y = (jnp.concatenate(outs, axis=-1) * scale_ref[...]).astype(BF16)
    out = jnp.dot(y, wout_ref[...], preferred_element_type=F32)
    o_ref[0] = x_ref[0] + g1_ref[0] * out


def _odd_post(x, u, w_grp, ch_scale, w_out, g1, tm):
    b, s, d = x.shape
    hb = tm // POOL_HALO
    rows = POOL_PAD + POOL_HALO + tm
    return pl.pallas_call(
        functools.partial(_odd_post_kernel, tm=tm),
        grid=(b, s // tm),
        in_specs=[
            pl.BlockSpec((1, tm, d), lambda bi, i: (bi, i, 0)),
            pl.BlockSpec((1, tm, d), lambda bi, i: (bi, i, 0)),
            pl.BlockSpec((1, POOL_HALO, d), lambda bi, i: (bi, jnp.maximum(i * hb - 1, 0), 0)),
            pl.BlockSpec(w_grp.shape, lambda bi, i: (0, 0, 0)),
            pl.BlockSpec((1, d), lambda bi, i: (0, 0)),
            pl.BlockSpec(w_out.shape, lambda bi, i: (0, 0)),
            pl.BlockSpec((1, 1, d), lambda bi, i: (bi, 0, 0)),
        ],
        out_specs=pl.BlockSpec((1, tm, d), lambda bi, i: (bi, i, 0)),
        out_shape=jax.ShapeDtypeStruct((b, s, d), F32),
        scratch_shapes=[pltpu.VMEM((rows, d), F32)] * 4,
        compiler_params=_params("parallel", "parallel"),
        name="odd_post",
    )(x, u, u, w_grp, ch_scale.reshape(1, d), w_out, g1)


def _route_kernel(x_ref, g_ref, sh_ref, sc_ref, wr_ref, bias_ref, hT_ref, wt_ref, *, n_exp):
    h = _norm_mod(x_ref[0], g_ref[...], sh_ref[0], sc_ref[0])
    h_hi = h.astype(BF16)
    h_lo = (h - h_hi.astype(F32)).astype(BF16)
    hT_ref[...] = h.T.astype(BF16)
    nt = (((1,), (1,)), ((), ()))
    both = lax.dot_general(wr_ref[...], h_hi, nt, preferred_element_type=F32)
    logits = (both[:n_exp] + both[n_exp:]
              + lax.dot_general(wr_ref[0:n_exp, :], h_lo, nt, preferred_element_type=F32))
    scores = _sigmoid(logits)
    biased = scores + bias_ref[...]
    per = n_exp // N_GROUPS
    tm = scores.shape[1]
    sub = lax.broadcasted_iota(jnp.int32, (per, tm), 0)
    slabs = [biased[g * per:(g + 1) * per, :] for g in range(N_GROUPS)]
    gs = []
    for sl in slabs:
        m1 = jnp.max(sl, axis=0, keepdims=True)
        first = jnp.min(jnp.where(sl == m1, sub, per), axis=0, keepdims=True)
        m2 = jnp.max(jnp.where(sub == first, -jnp.inf, sl), axis=0, keepdims=True)
        gs.append(m1 + m2)
    masked = []
    for g in range(N_GROUPS):
        rank = jnp.zeros((1, tm), jnp.int32)
        for g2 in range(N_GROUPS):
            if g2 < g:
                rank = rank + (gs[g2] >= gs[g]).astype(jnp.int32)
            elif g2 > g:
                rank = rank + (gs[g2] > gs[g]).astype(jnp.int32)
        masked.append(jnp.where(rank < TOPK_GROUPS, slabs[g], -jnp.inf))
    ranks = [jnp.zeros((per, tm), jnp.int32) for _ in range(N_GROUPS)]
    for g2 in range(N_GROUPS):
        for i2 in range(per):
            r = masked[g2][i2:i2 + 1, :]
            for g in range(N_GROUPS):
                if g2 < g:
                    beats = r >= masked[g]
                elif g2 > g:
                    beats = r > masked[g]
                else:
                    beats = (r > masked[g]) | ((r == masked[g]) & (sub > i2))
                ranks[g] = ranks[g] + beats.astype(jnp.int32)
    picked = [jnp.where(ranks[g] < TOP_K, scores[g * per:(g + 1) * per, :], 0.0)
              for g in range(N_GROUPS)]
    tot = picked[0]
    for g in range(1, N_GROUPS):
        tot = tot + picked[g]
    denom = jnp.sum(tot, axis=0, keepdims=True)
    for g in range(N_GROUPS):
        wt_ref[g * per:(g + 1) * per, :] = picked[g] / denom * ROUTED_SCALE
    tail = wt_ref.shape[0] - n_exp
    wt_ref[n_exp:, :] = (lax.broadcasted_iota(jnp.int32, (tail, tm), 0) == 0).astype(F32)


def _route(x, g, shift, scale, wr2, bias, tm):
    b, s, d = x.shape
    n = b * s
    n_exp = bias.shape[0]
    per_b = s // tm
    rows = -(-(n_exp + 1) // 8) * 8
    return pl.pallas_call(
        functools.partial(_route_kernel, n_exp=n_exp),
        grid=(b, per_b),
        in_specs=[
            pl.BlockSpec((1, tm, d), lambda bi, i: (bi, i, 0)),
            pl.BlockSpec((1, d), lambda bi, i: (0, 0)),
            pl.BlockSpec((1, 1, d), lambda bi, i: (bi, 0, 0)),
            pl.BlockSpec((1, 1, d), lambda bi, i: (bi, 0, 0)),
            pl.BlockSpec(wr2.shape, lambda bi, i: (0, 0)),
            pl.BlockSpec((n_exp, 1), lambda bi, i: (0, 0)),
        ],
        out_specs=[
            pl.BlockSpec((d, tm), lambda bi, i: (0, bi * per_b + i)),
            pl.BlockSpec((rows, tm), lambda bi, i: (0, bi * per_b + i)),
        ],
        out_shape=[
            jax.ShapeDtypeStruct((d, n), BF16),
            jax.ShapeDtypeStruct((rows, n), F32),
        ],
        compiler_params=_params("parallel", "parallel"),
        name="moe_route",
    )(x, g.reshape(1, d), shift, scale, wr2, bias.reshape(n_exp, 1))


def _moe_dense_kernel(hT_ref, wt_ref, w1_ref, w3_ref, w2_ref, x_ref, g2_ref, o_ref, acc_ref):
    e = pl.program_id(1)

    @pl.when(e == 0)
    def _():
        acc_ref[...] = jnp.zeros_like(acc_ref)

    hT = hT_ref[...]
    a = jnp.dot(w1_ref[0], hT, preferred_element_type=F32)
    bb = jnp.dot(w3_ref[0], hT, preferred_element_type=F32)
    gated = (a * _sigmoid(a)) * bb * wt_ref[0]
    acc_ref[...] += jnp.dot(w2_ref[0], gated.astype(BF16), preferred_element_type=F32)

    @pl.when(e == pl.num_programs(1) - 1)
    def _():
        o_ref[0] = x_ref[0] + g2_ref[0] * acc_ref[...].T


def _moe_dense(hT, wt, w1T, w3T, w2T, x, g2, tm):
    b, s, d = x.shape
    n = b * s
    ne = w1T.shape[0]
    f = w1T.shape[1]
    per_b = s // tm
    wt3 = wt[:ne].reshape(ne, 1, n)
    return pl.pallas_call(
        _moe_dense_kernel,
        grid=(n // tm, ne),
        in_specs=[
            pl.BlockSpec((d, tm), lambda i, e: (0, i)),
            pl.BlockSpec((1, 1, tm), lambda i, e: (e, 0, i)),
            pl.BlockSpec((1, f, d), lambda i, e: (e, 0, 0)),
            pl.BlockSpec((1, f, d), lambda i, e: (e, 0, 0)),
            pl.BlockSpec((1, d, f), lambda i, e: (e, 0, 0)),
            pl.BlockSpec((1, tm, d), lambda i, e: (i // per_b, i % per_b, 0)),
            pl.BlockSpec((1, 1, d), lambda i, e: (i // per_b, 0, 0)),
        ],
        out_specs=pl.BlockSpec((1, tm, d), lambda i, e: (i // per_b, i % per_b, 0)),
        out_shape=jax.ShapeDtypeStruct((b, s, d), F32),
        scratch_shapes=[pltpu.VMEM((d, tm), F32)],
        compiler_params=_params("parallel", "arbitrary"),
        name="moe_dense",
    )(hT, wt3, w1T, w3T, w2T, x, g2)


def _final_kernel(x_ref, g_ref, o_ref):
    x = x_ref[0]
    ms = jnp.mean(x * x, axis=-1, keepdims=True)
    o_ref[0] = x * lax.rsqrt(ms + EPS) * g_ref[...]


def _final_norm(x, g, tm):
    b, s, d = x.shape
    return pl.pallas_call(
        _final_kernel,
        grid=(b, s // tm),
        in_specs=[
            pl.BlockSpec((1, tm, d), lambda bi, i: (bi, i, 0)),
            pl.BlockSpec((1, d), lambda bi, i: (0, 0)),
        ],
        out_specs=pl.BlockSpec((1, tm, d), lambda bi, i: (bi, i, 0)),
        out_shape=jax.ShapeDtypeStruct((b, s, d), F32),
        compiler_params=_params("parallel", "parallel"),
        name="final_norm",
    )(x, g.reshape(1, d))


def _tile(s, want):
    t = min(s, want)
    assert s % t == 0, (s, t)
    return t


def kernel(x, c, ada_w, ada_b, norm1_g, norm2_g, ev_w_in, ev_dw_w, ev_dw_b, ev_ln_g, ev_ln_b,
           ev_w_out, od_w_in, od_w_grp, od_scale, od_w_out, moe_w_router, moe_bias, moe_w1, moe_w3,
           moe_w2, sh_w1, sh_w3, sh_w2, final_g):
    b, s, d = x.shape
    depth = ada_w.shape[0]
    tm = _tile(s, 512)
    tq = _tile(s, 512)
    tk = _tile(s, 256)
    tm_moe = _tile(s, 1024)

    mod = _ada(c, ada_w, ada_b)
    for l in range(depth):
        sh1, sc1, g1, sh2, sc2, g2 = [m.reshape(b, 1, d) for m in jnp.split(mod[l], 6, axis=-1)]
        i = l // 2
        if l % 2 == 0:
            qkv, u = _even_in(x, norm1_g[l], sh1, sc1, ev_w_in[i].astype(BF16), tm)
            att = _sb_attention(qkv, tq, tk)
            x = _even_post(x, att, u, ev_dw_w[i], ev_dw_b[i], ev_ln_g[i], ev_ln_b[i],
                           ev_w_out[i].astype(BF16), g1, tm)
        else:
            u = _odd_in(x, norm1_g[l], sh1, sc1, od_w_in[i].astype(BF16), tm)
            x = _odd_post(x, u, od_w_grp[i].astype(BF16), od_scale[i], od_w_out[i].astype(BF16),
                          g1, _tile(s, 256))
        wr = moe_w_router[l].T
        wr_hi = wr.astype(BF16)
        wr_lo = (wr - wr_hi.astype(F32)).astype(BF16)
        hT, wt = _route(x, norm2_g[l], sh2, sc2, jnp.concatenate([wr_hi, wr_lo], axis=0),
                        moe_bias[l], tm)
        w1T = jnp.concatenate([moe_w1[l], sh_w1[l][None]], axis=0).transpose(0, 2, 1).astype(BF16)
        w3T = jnp.concatenate([moe_w3[l], sh_w3[l][None]], axis=0).transpose(0, 2, 1).astype(BF16)
        w2T = jnp.concatenate([moe_w2[l], sh_w2[l][None]], axis=0).transpose(0, 2, 1).astype(BF16)
        x = _moe_dense(hT, wt, w1T, w3T, w2T, x, g2, tm_moe)
    return _final_norm(x, final_g, tm)
```

```python
import functools

import jax
import jax.numpy as jnp
from jax import lax
from jax.experimental import pallas as pl
from jax.experimental.pallas import tpu as pltpu

F32 = jnp.float32
BF16 = jnp.bfloat16

EPS = 1e-6
SB_HEADS = 8
SB_HEAD_DIM = 64
SB_WIDTH = SB_HEADS * SB_HEAD_DIM
POOL_WINDOWS = (2, 4, 8, 16)
N_GROUPS = 8
TOPK_GROUPS = 4
TOP_K = 8
ROUTED_SCALE = 2.5

VMEM_LIMIT_BYTES = 56 * 1024 * 1024
LANES = 128


def _params(*semantics):
    return pltpu.CompilerParams(dimension_semantics=semantics, vmem_limit_bytes=VMEM_LIMIT_BYTES)


def _sigmoid(x):
    return 1.0 / (1.0 + jnp.exp(-x))


def _norm_mod(x, g, shift, scale):
    ms = jnp.mean(x * x, axis=-1, keepdims=True)
    y = x * lax.rsqrt(ms + EPS) * g
    return y * (1.0 + scale) + shift


def _ada_kernel(c_ref, w_ref, b_ref, o_ref):
    c = c_ref[...]
    cond = c * _sigmoid(c)
    o_ref[0] = jnp.dot(cond, w_ref[0], precision=lax.Precision.HIGHEST,
                       preferred_element_type=F32) + b_ref[0]


def _ada(c, ada_w, ada_b):
    depth, d, n6 = ada_w.shape
    b = c.shape[0]
    tn = 1536
    return pl.pallas_call(
        _ada_kernel,
        grid=(depth, n6 // tn),
        in_specs=[
            pl.BlockSpec((b, d), lambda l, j: (0, 0)),
            pl.BlockSpec((1, d, tn), lambda l, j: (l, 0, j)),
            pl.BlockSpec((1, 1, tn), lambda l, j: (l, 0, j)),
        ],
        out_specs=pl.BlockSpec((1, b, tn), lambda l, j: (l, 0, j)),
        out_shape=jax.ShapeDtypeStruct((depth, b, n6), F32),
        compiler_params=_params("parallel", "parallel"),
        name="ada_mod",
    )(c, ada_w, ada_b.reshape(depth, 1, n6))


def _even_in_kernel(x_ref, g_ref, sh_ref, sc_ref, w_ref, qkv_ref, u_ref):
    h = _norm_mod(x_ref[0], g_ref[...], sh_ref[0], sc_ref[0]).astype(BF16)
    n_qkv = qkv_ref.shape[-1]
    cw = u_ref.shape[-1]
    for j in range(n_qkv // cw):
        qkv_ref[0, :, j * cw:(j + 1) * cw] = jnp.dot(
            h, w_ref[:, j * cw:(j + 1) * cw], preferred_element_type=F32).astype(BF16)
    val = jnp.dot(h, w_ref[:, n_qkv:n_qkv + cw], preferred_element_type=F32)
    gate = jnp.dot(h, w_ref[:, n_qkv + cw:], preferred_element_type=F32)
    u_ref[0] = val * _sigmoid(gate)


def _even_in(x, g, shift, scale, w_in, tm):
    b, s, d = x.shape
    n_qkv = 3 * SB_WIDTH
    cw = (w_in.shape[1] - n_qkv) // 2
    return pl.pallas_call(
        _even_in_kernel,
        grid=(b, s // tm),
        in_specs=[
            pl.BlockSpec((1, tm, d), lambda bi, i: (bi, i, 0)),
            pl.BlockSpec((1, d), lambda bi, i: (0, 0)),
            pl.BlockSpec((1, 1, d), lambda bi, i: (bi, 0, 0)),
            pl.BlockSpec((1, 1, d), lambda bi, i: (bi, 0, 0)),
            pl.BlockSpec(w_in.shape, lambda bi, i: (0, 0)),
        ],
        out_specs=[
            pl.BlockSpec((1, tm, n_qkv), lambda bi, i: (bi, i, 0)),
            pl.BlockSpec((1, tm, cw), lambda bi, i: (bi, i, 0)),
        ],
        out_shape=[
            jax.ShapeDtypeStruct((b, s, n_qkv), BF16),
            jax.ShapeDtypeStruct((b, s, cw), F32),
        ],
        compiler_params=_params("parallel", "parallel"),
        name="even_in",
    )(x, g.reshape(1, d), shift, scale, w_in)


SB_UNDERFLOW_MASS = 110.0


def _sb_kernel(qT_ref, k_ref, vT_ref, tri_ref, o_ref, *scratch, t):
    qi = pl.program_id(2)
    nh = qT_ref.shape[1]
    acc_refs, r_refs = scratch[:nh], scratch[nh:]
    tri = tri_ref[...]

    def blocks(j, valid):
        heads = range(nh)
        zs = [jnp.dot(k_ref[0, hd, j], qT_ref[0, hd], preferred_element_type=F32) for hd in heads]
        sps, ds = [], []
        for zT in zs:
            sp = jnp.maximum(zT, 0.0) + jnp.log(1.0 + jnp.exp(-jnp.abs(zT)))
            if valid is not None:
                sp = jnp.where(valid, sp, 0.0)
            sps.append(sp)
            ds.append(zT - sp)
        laters = [jnp.dot(tri, sp.astype(BF16), preferred_element_type=F32) for sp in sps]
        r_min = None
        for hd in heads:
            r_prev = r_refs[hd][...]
            a = jnp.exp(ds[hd] - laters[hd] - r_prev)
            if valid is not None:
                a = jnp.where(valid, a, 0.0)
            acc_refs[hd][...] += jnp.dot(vT_ref[0, hd, j], a.astype(BF16),
                                         preferred_element_type=F32)
            r_new = r_prev + laters[hd][0:1, :] + sps[hd][0:1, :]
            r_refs[hd][...] = r_new
            m = jnp.min(r_new)
            r_min = m if r_min is None else jnp.minimum(r_min, m)
        return r_min

    row = lax.broadcasted_iota(jnp.int32, (t, t), 0)
    col = lax.broadcasted_iota(jnp.int32, (t, t), 1)
    for hd in range(nh):
        acc_refs[hd][...] = jnp.zeros(acc_refs[hd].shape, F32)
        r_refs[hd][...] = jnp.zeros(r_refs[hd].shape, F32)
    r_min = blocks(qi, row < col)

    def cond(carry):
        j, r_min = carry
        return jnp.logical_and(j >= 0, r_min < SB_UNDERFLOW_MASS)

    def body(carry):
        j, _ = carry
        return j - 1, blocks(j, None)

    lax.while_loop(cond, body, (qi - 1, r_min))
    for hd in range(nh):
        o_ref[0, hd] = acc_refs[hd][...].astype(o_ref.dtype)


def _sb_attention(qkv, t, nh):
    b, s, _ = qkv.shape
    h, dh = SB_HEADS, SB_HEAD_DIM
    nkb = s // t
    q = qkv[..., :SB_WIDTH].reshape(b, s, h, dh) * (dh ** -0.5)
    k = qkv[..., SB_WIDTH:2 * SB_WIDTH].reshape(b, s, h, dh)
    v = qkv[..., 2 * SB_WIDTH:].reshape(b, s, h, dh)
    pad = LANES - dh
    qT = jnp.pad(q.transpose(0, 2, 3, 1), ((0, 0), (0, 0), (0, pad), (0, 0)))
    kb = jnp.pad(k.transpose(0, 2, 1, 3), ((0, 0), (0, 0), (0, 0), (0, pad)))
    kb = kb.reshape(b, h, nkb, t, LANES)
    vT = v.transpose(0, 2, 3, 1).reshape(b, h, dh, nkb, t).transpose(0, 1, 3, 2, 4)
    tri = (jnp.arange(t)[None, :] > jnp.arange(t)[:, None]).astype(BF16)
    oT = pl.pallas_call(
        functools.partial(_sb_kernel, t=t),
        grid=(b, h // nh, s // t),
        in_specs=[
            pl.BlockSpec((1, nh, LANES, t), lambda bi, hi, qi: (bi, hi, 0, qi)),
            pl.BlockSpec((1, nh, nkb, t, LANES), lambda bi, hi, qi: (bi, hi, 0, 0, 0)),
            pl.BlockSpec((1, nh, nkb, dh, t), lambda bi, hi, qi: (bi, hi, 0, 0, 0)),
            pl.BlockSpec((t, t), lambda bi, hi, qi: (0, 0)),
        ],
        out_specs=pl.BlockSpec((1, nh, dh, t), lambda bi, hi, qi: (bi, hi, 0, qi)),
        out_shape=jax.ShapeDtypeStruct((b, h, dh, s), BF16),
        scratch_shapes=[pltpu.VMEM((dh, t), F32)] * nh + [pltpu.VMEM((1, t), F32)] * nh,
        compiler_params=_params("parallel", "parallel", "arbitrary"),
        name="sb_attention",
    )(qT, kb, vT, tri)
    return oT.transpose(0, 3, 1, 2).reshape(b, s, SB_WIDTH)


CONV_HALO = 32


def _even_post_kernel(x_ref, att_ref, ucur_ref, uhalo_ref, dww_ref, dwb_ref, lng_ref, lnb_ref,
                      wout_ref, g1_ref, o_ref, ext_ref, *, tm, taps):
    i = pl.program_id(1)
    ext_ref[0:CONV_HALO, :] = jnp.where(i > 0, uhalo_ref[0], 0.0)
    ext_ref[CONV_HALO:CONV_HALO + tm, :] = ucur_ref[0]
    base = CONV_HALO - (taps - 1)
    acc = ext_ref[base:base + tm, :] * dww_ref[0:1, :]
    for k in range(1, taps):
        acc = acc + ext_ref[base + k:base + k + tm, :] * dww_ref[k:k + 1, :]
    conv = acc + dwb_ref[...]
    mu = jnp.mean(conv, axis=-1, keepdims=True)
    cen = conv - mu
    var = jnp.mean(cen * cen, axis=-1, keepdims=True)
    y = cen * lax.rsqrt(var + EPS) * lng_ref[...] + lnb_ref[...]
    c = (y * _sigmoid(y)).astype(BF16)
    cat = jnp.concatenate([att_ref[0], c], axis=-1)
    out = jnp.dot(cat, wout_ref[...], preferred_element_type=F32)
    o_ref[0] = x_ref[0] + g1_ref[0] * out


def _even_post(x, att, u, dw_w, dw_b, ln_g, ln_b, w_out, g1, tm):
    b, s, d = x.shape
    cw = u.shape[-1]
    taps = dw_w.shape[0]
    hb = tm // CONV_HALO
    return pl.pallas_call(
        functools.partial(_even_post_kernel, tm=tm, taps=taps),
        grid=(b, s // tm),
        in_specs=[
            pl.BlockSpec((1, tm, d), lambda bi, i: (bi, i, 0)),
            pl.BlockSpec((1, tm, att.shape[-1]), lambda bi, i: (bi, i, 0)),
            pl.BlockSpec((1, tm, cw), lambda bi, i: (bi, i, 0)),
            pl.BlockSpec((1, CONV_HALO, cw), lambda bi, i: (bi, jnp.maximum(i * hb - 1, 0), 0)),
            pl.BlockSpec((taps, cw), lambda bi, i: (0, 0)),
            pl.BlockSpec((1, cw), lambda bi, i: (0, 0)),
            pl.BlockSpec((1, cw), lambda bi, i: (0, 0)),
            pl.BlockSpec((1, cw), lambda bi, i: (0, 0)),
            pl.BlockSpec(w_out.shape, lambda bi, i: (0, 0)),
            pl.BlockSpec((1, 1, d), lambda bi, i: (bi, 0, 0)),
        ],
        out_specs=pl.BlockSpec((1, tm, d), lambda bi, i: (bi, i, 0)),
        out_shape=jax.ShapeDtypeStruct((b, s, d), F32),
        scratch_shapes=[pltpu.VMEM((CONV_HALO + tm, cw), F32)],
        compiler_params=_params("parallel", "parallel"),
        name="even_post",
    )(x, att, u, u, dw_w.reshape(taps, cw), dw_b.reshape(1, cw), ln_g.reshape(1, cw),
      ln_b.reshape(1, cw), w_out, g1)


def _odd_in_kernel(x_ref, g_ref, sh_ref, sc_ref, w_ref, u_ref):
    h = _norm_mod(x_ref[0], g_ref[...], sh_ref[0], sc_ref[0]).astype(BF16)
    u_ref[0] = jnp.dot(h, w_ref[...], preferred_element_type=F32)


def _odd_in(x, g, shift, scale, w_in, tm):
    b, s, d = x.shape
    return pl.pallas_call(
        _odd_in_kernel,
        grid=(b, s // tm),
        in_specs=[
            pl.BlockSpec((1, tm, d), lambda bi, i: (bi, i, 0)),
            pl.BlockSpec((1, d), lambda bi, i: (0, 0)),
            pl.BlockSpec((1, 1, d), lambda bi, i: (bi, 0, 0)),
            pl.BlockSpec((1, 1, d), lambda bi, i: (bi, 0, 0)),
            pl.BlockSpec(w_in.shape, lambda bi, i: (0, 0)),
        ],
        out_specs=pl.BlockSpec((1, tm, w_in.shape[1]), lambda bi, i: (bi, i, 0)),
        out_shape=jax.ShapeDtypeStruct((b, s, w_in.shape[1]), F32),
        compiler_params=_params("parallel", "parallel"),
        name="odd_in",
    )(x, g.reshape(1, d), shift, scale, w_in)


POOL_HALO = 16
POOL_PAD = 8


def _odd_post_kernel(x_ref, ucur_ref, uhalo_ref, wgrp_ref, scale_ref, wout_ref, g1_ref, o_ref,
                     e_ref, s1_ref, s2_ref, s4_ref, *, tm):
    i = pl.program_id(1)
    d = e_ref.shape[1]
    gw = d // len(POOL_WINDOWS)
    ext = POOL_HALO + tm
    lo = POOL_PAD
    zeros_pad = jnp.zeros((POOL_PAD, d), F32)
    e_ref[0:lo, :] = zeros_pad
    e_ref[lo:lo + POOL_HALO, :] = jnp.where(i > 0, uhalo_ref[0], 0.0)
    e_ref[lo + POOL_HALO:lo + ext, :] = ucur_ref[0]
    s1_ref[0:lo, :] = zeros_pad
    s1_ref[lo:lo + ext, :] = e_ref[lo:lo + ext, :] + e_ref[lo - 1:lo - 1 + ext, :]
    s2_ref[0:lo, :] = zeros_pad
    s2_ref[lo:lo + ext, :] = s1_ref[lo:lo + ext, :] + s1_ref[lo - 2:lo - 2 + ext, :]
    s4_ref[0:lo, :] = zeros_pad
    s4_ref[lo:lo + ext, :] = s2_ref[lo:lo + ext, :] + s2_ref[lo - 4:lo - 4 + ext, :]
    t0 = lo + POOL_HALO
    pos = i * tm + lax.broadcasted_iota(jnp.int32, (tm, 1), 0) + 1
    sums = (
        s1_ref[t0:t0 + tm, 0:gw],
        s2_ref[t0:t0 + tm, gw:2 * gw],
        s4_ref[t0:t0 + tm, 2 * gw:3 * gw],
        s4_ref[t0:t0 + tm, 3 * gw:4 * gw] + s4_ref[t0 - 8:t0 - 8 + tm, 3 * gw:4 * gw],
    )
    outs = []
    for gi, w in enumerate(POOL_WINDOWS):
        cnt = jnp.minimum(pos, w).astype(F32)
        ug = e_ref[t0:t0 + tm, gi * gw:(gi + 1) * gw]
        pooled = sums[gi] / cnt - ug
        outs.append(jnp.dot(pooled.astype(BF16), wgrp_ref[gi], preferred_element_type=F32))
    y = (jnp.concatenate(outs, axis=-1) * scale_ref[...]).astype(BF16)
    out = jnp.dot(y, wout_ref[...], preferred_element_type=F32)
    o_ref[0] = x_ref[0] + g1_ref[0] * out


def _odd_post(x, u, w_grp, ch_scale, w_out, g1, tm):
    b, s, d = x.shape
    hb = tm // POOL_HALO
    rows = POOL_PAD + POOL_HALO + tm
    return pl.pallas_call(
        functools.partial(_odd_post_kernel, tm=tm),
        grid=(b, s // tm),
        in_specs=[
            pl.BlockSpec((1, tm, d), lambda bi, i: (bi, i, 0)),
            pl.BlockSpec((1, tm, d), lambda bi, i: (bi, i, 0)),
            pl.BlockSpec((1, POOL_HALO, d), lambda bi, i: (bi, jnp.maximum(i * hb - 1, 0), 0)),
            pl.BlockSpec(w_grp.shape, lambda bi, i: (0, 0, 0)),
            pl.BlockSpec((1, d), lambda bi, i: (0, 0)),
            pl.BlockSpec(w_out.shape, lambda bi, i: (0, 0)),
            pl.BlockSpec((1, 1, d), lambda bi, i: (bi, 0, 0)),
        ],
        out_specs=pl.BlockSpec((1, tm, d), lambda bi, i: (bi, i, 0)),
        out_shape=jax.ShapeDtypeStruct((b, s, d), F32),
        scratch_shapes=[pltpu.VMEM((rows, d), F32)] * 4,
        compiler_params=_params("parallel", "parallel"),
        name="odd_post",
    )(x, u, u, w_grp, ch_scale.reshape(1, d), w_out, g1)


def _route_kernel(x_ref, g_ref, sh_ref, sc_ref, wr_ref, bias_ref, h_ref, wt_ref, cnt_ref, *, n_exp):
    h = _norm_mod(x_ref[0], g_ref[...], sh_ref[0], sc_ref[0])
    h_hi = h.astype(BF16)
    h_lo = (h - h_hi.astype(F32)).astype(BF16)
    h_ref[...] = h_hi
    nt = (((1,), (1,)), ((), ()))
    both = lax.dot_general(wr_ref[...], h_hi, nt, preferred_element_type=F32)
    logits = (both[:n_exp] + both[n_exp:]
              + lax.dot_general(wr_ref[0:n_exp, :], h_lo, nt, preferred_element_type=F32))
    scores = _sigmoid(logits)
    biased = scores + bias_ref[...]
    per = n_exp // N_GROUPS
    tm = scores.shape[1]
    sub = lax.broadcasted_iota(jnp.int32, (per, tm), 0)
    slabs = [biased[g * per:(g + 1) * per, :] for g in range(N_GROUPS)]
    gs = []
    for sl in slabs:
        m1 = jnp.max(sl, axis=0, keepdims=True)
        first = jnp.min(jnp.where(sl == m1, sub, per), axis=0, keepdims=True)
        m2 = jnp.max(jnp.where(sub == first, -jnp.inf, sl), axis=0, keepdims=True)
        gs.append(m1 + m2)
    masked = []
    for g in range(N_GROUPS):
        rank = jnp.zeros((1, tm), jnp.int32)
        for g2 in range(N_GROUPS):
            if g2 < g:
                rank = rank + (gs[g2] >= gs[g]).astype(jnp.int32)
            elif g2 > g:
                rank = rank + (gs[g2] > gs[g]).astype(jnp.int32)
        masked.append(jnp.where(rank < TOPK_GROUPS, slabs[g], -jnp.inf))
    ranks = [jnp.zeros((per, tm), jnp.int32) for _ in range(N_GROUPS)]
    for g2 in range(N_GROUPS):
        for i2 in range(per):
            r = masked[g2][i2:i2 + 1, :]
            for g in range(N_GROUPS):
                if g2 < g:
                    beats = r >= masked[g]
                elif g2 > g:
                    beats = r > masked[g]
                else:
                    beats = (r > masked[g]) | ((r == masked[g]) & (sub > i2))
                ranks[g] = ranks[g] + beats.astype(jnp.int32)
    picked = [jnp.where(ranks[g] < TOP_K, scores[g * per:(g + 1) * per, :], 0.0)
              for g in range(N_GROUPS)]
    tot = picked[0]
    for g in range(1, N_GROUPS):
        tot = tot + picked[g]
    denom = jnp.sum(tot, axis=0, keepdims=True)
    for g in range(N_GROUPS):
        wt_ref[g * per:(g + 1) * per, :] = picked[g] / denom * ROUTED_SCALE
    ones = jnp.ones((MOE_SUB, LANES), BF16)
    for k in range(tm // MOE_SUB):
        sel = jnp.where(wt_ref[:, k * MOE_SUB:(k + 1) * MOE_SUB] > 0.0, 1.0, 0.0).astype(BF16)
        cnt_ref[:, k * LANES:(k + 1) * LANES] = jnp.dot(sel, ones, preferred_element_type=F32)


def _route(x, g, shift, scale, wr2, bias, tm):
    b, s, d = x.shape
    n = b * s
    n_exp = bias.shape[0]
    per_b = s // tm
    cw = tm // MOE_SUB * LANES
    return pl.pallas_call(
        functools.partial(_route_kernel, n_exp=n_exp),
        grid=(b, per_b),
        in_specs=[
            pl.BlockSpec((1, tm, d), lambda bi, i: (bi, i, 0)),
            pl.BlockSpec((1, d), lambda bi, i: (0, 0)),
            pl.BlockSpec((1, 1, d), lambda bi, i: (bi, 0, 0)),
            pl.BlockSpec((1, 1, d), lambda bi, i: (bi, 0, 0)),
            pl.BlockSpec(wr2.shape, lambda bi, i: (0, 0)),
            pl.BlockSpec((n_exp, 1), lambda bi, i: (0, 0)),
        ],
        out_specs=[
            pl.BlockSpec((tm, d), lambda bi, i: (bi * per_b + i, 0)),
            pl.BlockSpec((n_exp, tm), lambda bi, i: (0, bi * per_b + i)),
            pl.BlockSpec((n_exp, cw), lambda bi, i: (0, bi * per_b + i)),
        ],
        out_shape=[
            jax.ShapeDtypeStruct((n, d), BF16),
            jax.ShapeDtypeStruct((n_exp, n), F32),
            jax.ShapeDtypeStruct((n_exp, n // MOE_SUB * LANES), F32),
        ],
        compiler_params=_params("parallel", "parallel"),
        name="moe_route",
    )(x, g.reshape(1, d), shift, scale, wr2, bias.reshape(n_exp, 1))


MOE_SUB = 256
MOE_TILE = 16
MOE_ROWBLK = 512
MOE_CHUNK = 512
TILES_PER_CHUNK = MOE_CHUNK // MOE_TILE
TILES_PER_BLK = MOE_ROWBLK // MOE_TILE
TAB_W = 256


def _max_tiles(n_exp):
    worst = (TOP_K * MOE_SUB + n_exp * (MOE_TILE - 1)) // MOE_TILE
    return -(-worst // TILES_PER_CHUNK) * TILES_PER_CHUNK


def _dispatch_tables(cnt, n_sub, n_exp):
    i32 = jnp.int32
    nt_max = _max_tiles(n_exp)
    blk = MOE_ROWBLK // MOE_TILE
    n = cnt[:, ::LANES].T.astype(i32)
    np16 = (n + (MOE_TILE - 1)) // MOE_TILE
    q_end = jnp.cumsum(np16, axis=1)
    q_off = q_end - np16
    n_tiles = q_end[:, -1]
    g_off = jnp.cumsum(np16, axis=0) - np16
    tot = jnp.sum(np16, axis=0)
    tot_r = (tot + (blk - 1)) // blk * blk
    bend = jnp.cumsum(tot_r)
    base = bend - tot_r
    j = jnp.arange(nt_max, dtype=i32)
    e_of = jnp.minimum(jnp.sum(j[None, :, None] >= q_end[:, None, :], axis=-1), n_exp - 1)
    dest = (jnp.take_along_axis(base[None, :] + g_off, e_of, axis=1)
            + j[None, :] - jnp.take_along_axis(q_off, e_of, axis=1))
    dest = jnp.where(j[None, :] < n_tiles[:, None], dest, 0)
    tab = jnp.concatenate(
        [n_tiles[:, None], dest, jnp.zeros((n_sub, TAB_W - 1 - nt_max), i32)], axis=1)
    n_blocks = (TOP_K * n_sub * MOE_SUB + n_sub * n_exp * (MOE_TILE - 1)
                + n_exp * (MOE_ROWBLK - 1)) // MOE_ROWBLK + 1
    n_used = (bend[-1] // blk).astype(i32).reshape(1)
    gaps = jnp.stack([base + tot, tot_r - tot, jnp.broadcast_to(n_used, (n_exp,))]).astype(i32)
    b_idx = jnp.minimum(jnp.arange(n_blocks, dtype=i32), n_used[0] - 1)
    block_e = jnp.minimum(jnp.sum(b_idx[:, None] * blk >= bend[None, :], axis=1), n_exp - 1)
    row_lo = (q_off * MOE_TILE).astype(F32)
    row_hi = (q_end * MOE_TILE).astype(F32)
    pad_l = ((0, 0), (0, LANES - n_exp))
    lane_tabs = (jnp.pad(row_lo, pad_l).reshape(n_sub, 1, LANES),
                 jnp.pad(row_hi, pad_l).reshape(n_sub, 1, LANES))
    pad_s = ((0, 0), (0, LANES - n_exp), (0, 0))
    sub_tabs = (jnp.pad(jnp.broadcast_to(row_lo[:, :, None], (n_sub, n_exp, LANES)), pad_s),
                jnp.pad(jnp.broadcast_to(row_hi[:, :, None], (n_sub, n_exp, LANES)), pad_s))
    q_row = jnp.pad(q_off.astype(F32), pad_l)
    q_row8 = jnp.broadcast_to(q_row[:, None, :], (n_sub, 8, LANES)).astype(BF16)
    q_col = jnp.broadcast_to(q_off.astype(F32)[:, :, None], (n_sub, n_exp, LANES)).astype(BF16)
    return dict(tab=tab.reshape(n_sub, 1, TAB_W), gaps=gaps, n_blocks=n_blocks, n_used=n_used,
                block_e=block_e.astype(i32), lane_tabs=lane_tabs, sub_tabs=sub_tabs,
                q_row8=q_row8, q_col=q_col)


def _tile_rows(tile):
    if isinstance(tile, int):
        return pl.ds(tile * MOE_TILE, MOE_TILE)
    return pl.ds(pl.multiple_of(tile * MOE_TILE, MOE_TILE), MOE_TILE)


def _tile_copy(src_ref, src_tile, dst_ref, dst_tile, sem):
    return pltpu.make_async_copy(src_ref.at[_tile_rows(src_tile)], dst_ref.at[_tile_rows(dst_tile)],
                                 sem)


def _dispatch_kernel(tab_ref, gap_ref, h_ref, wt_ref, lo_ref, hi_ref, qcol_ref, su_ref, xg_ref,
                     stage_ref, zero_ref, sem, *, n_exp, d, n_blocks):
    s = pl.program_id(0)
    spare_tile0 = n_blocks * TILES_PER_BLK
    n_blk_total = xg_ref.shape[0] // MOE_ROWBLK
    n_tiles = tab_ref[0, 0, 0]
    n_chunks = (n_tiles + TILES_PER_CHUNK - 1) // TILES_PER_CHUNK
    wt = wt_ref[...]
    sel = wt > 0.0
    pos = jnp.dot(jnp.where(sel, 1.0, 0.0).astype(BF16), su_ref[...], preferred_element_type=F32)
    posm = jnp.where(sel, pos, -1.0).astype(BF16)
    wt_hi = wt.astype(BF16)
    wt_lo = (wt - wt_hi.astype(F32)).astype(BF16)
    top = jnp.concatenate([posm, qcol_ref[0], wt_hi, wt_lo], axis=1)
    rhs = jnp.concatenate([top, jnp.zeros((LANES - n_exp, top.shape[1]), BF16)], axis=0)
    h = h_ref[...]
    row_lo = lo_ref[0]
    row_hi = hi_ref[0]
    lane = lax.broadcasted_iota(jnp.int32, (MOE_CHUNK, LANES), 1)
    sub = MOE_SUB

    def chunk(c, carry):
        r = (c * MOE_CHUNK + lax.broadcasted_iota(jnp.int32, (MOE_CHUNK, LANES), 0)).astype(F32)
        owner = jnp.where((r >= row_lo) & (r < row_hi), 1.0, 0.0).astype(BF16)
        g = jnp.dot(owner, rhs, preferred_element_type=F32)
        j = r - MOE_TILE * g[:, sub:sub + LANES]
        hit = g[:, 0:sub] == jnp.concatenate([j] * (sub // LANES), axis=1)
        rows = jnp.dot(jnp.where(hit, 1.0, 0.0).astype(BF16), h, preferred_element_type=F32)
        wg = g[:, sub + LANES:2 * sub + LANES] + g[:, 2 * sub + LANES:]
        w = jnp.sum(jnp.where(hit, wg, 0.0), axis=1, keepdims=True)
        w_hi = w.astype(BF16).astype(F32)
        w_lo = (w - w_hi).astype(BF16).astype(F32)
        r0 = pl.multiple_of(c * MOE_CHUNK, MOE_CHUNK)
        stage_ref[pl.ds(r0, MOE_CHUNK), 0:d] = rows.astype(BF16)
        stage_ref[pl.ds(r0, MOE_CHUNK), d:] = jnp.where(lane < LANES // 2, w_hi, w_lo).astype(BF16)
        for jt in range(TILES_PER_CHUNK):
            t = c * TILES_PER_CHUNK + jt
            dst = jnp.where(t < n_tiles, tab_ref[0, 0, 1 + t], spare_tile0 + t)
            _tile_copy(stage_ref, t, xg_ref, dst, sem.at[0]).start()
        return carry

    lax.fori_loop(0, n_chunks, chunk, 0)

    def drain(c, carry):
        for jt in range(TILES_PER_CHUNK):
            _tile_copy(stage_ref, 0, xg_ref, 0, sem.at[0]).wait()
        return carry

    lax.fori_loop(0, n_chunks, drain, 0)

    @pl.when(s == pl.num_programs(0) - 1)
    def _():
        zero_ref[...] = jnp.zeros_like(zero_ref)

        def gap_copy(dst_tile):
            return _tile_copy(zero_ref, 0, xg_ref, dst_tile, sem.at[1])

        def blk_copy(blk):
            r0 = blk * MOE_ROWBLK
            if not isinstance(blk, int):
                r0 = pl.multiple_of(r0, MOE_ROWBLK)
            return pltpu.make_async_copy(zero_ref, xg_ref.at[pl.ds(r0, MOE_ROWBLK)], sem.at[2])

        def per_expert(e, carry):
            def one(k, c2):
                gap_copy(gap_ref[0, e] + k).start()
                return c2
            return lax.fori_loop(0, gap_ref[1, e], one, carry)

        lax.fori_loop(0, n_exp, per_expert, 0)
        n_used = gap_ref[2, 0]

        def fill(blk, carry):
            blk_copy(blk).start()
            return carry

        lax.fori_loop(n_used, n_blk_total, fill, 0)

        def per_expert_wait(e, carry):
            def one(k, c2):
                gap_copy(0).wait()
                return c2
            return lax.fori_loop(0, gap_ref[1, e], one, carry)

        lax.fori_loop(0, n_exp, per_expert_wait, 0)

        def fill_wait(blk, carry):
            blk_copy(0).wait()
            return carry

        lax.fori_loop(n_used, n_blk_total, fill_wait, 0)


def _dispatch(h, wt, tabs, n_exp):
    n, d = h.shape
    n_sub = n // MOE_SUB
    xw = d + LANES
    stage_rows = _max_tiles(n_exp) * MOE_TILE
    su = (jnp.arange(MOE_SUB)[:, None] < jnp.arange(MOE_SUB)[None, :]).astype(BF16)
    smem = pltpu.SMEM
    total_rows = tabs["n_blocks"] * MOE_ROWBLK + -(-stage_rows // MOE_ROWBLK) * MOE_ROWBLK
    return pl.pallas_call(
        functools.partial(_dispatch_kernel, n_exp=n_exp, d=d, n_blocks=tabs["n_blocks"]),
        grid=(n_sub,),
        in_specs=[
            pl.BlockSpec((1, 1, TAB_W), lambda s: (s, 0, 0), memory_space=smem),
            pl.BlockSpec(memory_space=smem),
            pl.BlockSpec((MOE_SUB, d), lambda s: (s, 0)),
            pl.BlockSpec((n_exp, MOE_SUB), lambda s: (0, s)),
            pl.BlockSpec((1, 1, LANES), lambda s: (s, 0, 0)),
            pl.BlockSpec((1, 1, LANES), lambda s: (s, 0, 0)),
            pl.BlockSpec((1, n_exp, LANES), lambda s: (s, 0, 0)),
            pl.BlockSpec((MOE_SUB, MOE_SUB), lambda s: (0, 0)),
        ],
        out_specs=pl.BlockSpec(memory_space=pl.ANY),
        out_shape=jax.ShapeDtypeStruct((total_rows, xw), BF16),
        scratch_shapes=[
            pltpu.VMEM((stage_rows, xw), BF16),
            pltpu.VMEM((MOE_ROWBLK, xw), BF16),
            pltpu.SemaphoreType.DMA((3,)),
        ],
        compiler_params=_params("arbitrary"),
        name="moe_dispatch",
    )(tabs["tab"], tabs["gaps"], h, wt, tabs["lane_tabs"][0], tabs["lane_tabs"][1], tabs["q_col"],
      su)


def _expert_kernel(be_ref, nu_ref, x_ref, w13_ref, w2_ref, y_ref, *, d, f):
    @pl.when(pl.program_id(0) >= nu_ref[0])
    def _():
        y_ref[...] = jnp.zeros_like(y_ref)

    @pl.when(pl.program_id(0) < nu_ref[0])
    def _():
        x = x_ref[...]
        half = LANES // 2
        w = x[:, d:d + 1].astype(F32) + x[:, d + half:d + half + 1].astype(F32)
        ab = jnp.dot(x[:, 0:d], w13_ref[0], preferred_element_type=F32)
        a = ab[:, 0:f]
        gated = (a * _sigmoid(a)) * ab[:, f:] * w
        y_ref[...] = jnp.dot(gated.astype(BF16), w2_ref[0],
                             preferred_element_type=F32).astype(y_ref.dtype)


def _experts(xg, w13, w2, tabs):
    rows, xw = xg.shape
    n_exp, d, f2 = w13.shape
    f = f2 // 2

    def blk(b, be, nu):
        return jnp.minimum(b, nu[0] - 1)

    return pl.pallas_call(
        functools.partial(_expert_kernel, d=d, f=f),
        grid_spec=pltpu.PrefetchScalarGridSpec(
            num_scalar_prefetch=2,
            grid=(tabs["n_blocks"],),
            in_specs=[
                pl.BlockSpec((MOE_ROWBLK, xw), lambda b, be, nu: (blk(b, be, nu), 0)),
                pl.BlockSpec((1, d, f2), lambda b, be, nu: (be[b], 0, 0)),
                pl.BlockSpec((1, f, d), lambda b, be, nu: (be[b], 0, 0)),
            ],
            out_specs=pl.BlockSpec((MOE_ROWBLK, d), lambda b, be, nu: (b, 0)),
        ),
        out_shape=jax.ShapeDtypeStruct((tabs["n_blocks"] * MOE_ROWBLK, d), BF16),
        compiler_params=_params("arbitrary"),
        name="moe_experts",
    )(tabs["block_e"], tabs["n_used"], xg, w13, w2)


def _combine_kernel(tab_ref, y_ref, wtT_ref, lo_ref, hi_ref, q8_ref, sl_ref, h_ref, s13_ref,
                    s2_ref, x_ref, g2_ref, o_ref, ybuf_ref, acc_ref, sem, *, f):
    n_tiles = tab_ref[0, 0, 0]
    n_chunks = (n_tiles + TILES_PER_CHUNK - 1) // TILES_PER_CHUNK
    d = ybuf_ref.shape[1]

    def fetch(c, carry):
        for jt in range(TILES_PER_CHUNK):
            t = c * TILES_PER_CHUNK + jt
            _tile_copy(y_ref, tab_ref[0, 0, 1 + t], ybuf_ref, t, sem.at[0]).start()
        return carry

    lax.fori_loop(0, n_chunks, fetch, 0)

    ab = jnp.dot(h_ref[...], s13_ref[...], preferred_element_type=F32)
    a = ab[:, 0:f]
    acc_ref[...] = jnp.dot(((a * _sigmoid(a)) * ab[:, f:]).astype(BF16), s2_ref[...],
                           preferred_element_type=F32)

    sel = wtT_ref[...] > 0.0
    pos = jnp.dot(sl_ref[...], jnp.where(sel, 1.0, 0.0).astype(BF16), preferred_element_type=F32)
    posm = jnp.where(sel, pos, -1.0).astype(BF16)
    row_lo = jnp.concatenate([lo_ref[0]] * (MOE_CHUNK // LANES), axis=1)
    row_hi = jnp.concatenate([hi_ref[0]] * (MOE_CHUNK // LANES), axis=1)
    q8 = q8_ref[0]

    def drain(c, carry):
        for jt in range(TILES_PER_CHUNK):
            _tile_copy(y_ref, 0, ybuf_ref, 0, sem.at[0]).wait()
        return carry

    lax.fori_loop(0, n_chunks, drain, 0)

    def chunk(c, carry):
        r = (c * MOE_CHUNK + lax.broadcasted_iota(jnp.int32, (LANES, MOE_CHUNK), 1)).astype(F32)
        owner = jnp.where((r >= row_lo) & (r < row_hi), 1.0, 0.0).astype(BF16)
        g = jnp.dot(posm, owner, preferred_element_type=F32)
        q = jnp.dot(q8, owner, preferred_element_type=F32)
        j = r[0:1, :] - MOE_TILE * q[0:1, :]
        hit = jnp.where(g == j, 1.0, 0.0).astype(BF16)
        r0 = pl.multiple_of(c * MOE_CHUNK, MOE_CHUNK)
        acc_ref[...] += jnp.dot(hit, ybuf_ref[pl.ds(r0, MOE_CHUNK), :], preferred_element_type=F32)
        return carry

    lax.fori_loop(0, n_chunks, chunk, 0)
    o_ref[0] = x_ref[0] + g2_ref[0] * acc_ref[...]


def _combine(y, wt, tabs, h, s13, s2, x, g2):
    b, s, d = x.shape
    n = b * s
    n_exp = wt.shape[0]
    n_sub = n // MOE_SUB
    per_b = s // MOE_SUB
    f = s2.shape[0]
    wtT = jnp.pad(wt.T, ((0, 0), (0, LANES - n_exp)))
    sl = (jnp.arange(MOE_SUB)[None, :] < jnp.arange(MOE_SUB)[:, None]).astype(BF16)
    ybuf_rows = _max_tiles(n_exp) * MOE_TILE
    return pl.pallas_call(
        functools.partial(_combine_kernel, f=f),
        grid=(n_sub,),
        in_specs=[
            pl.BlockSpec((1, 1, TAB_W), lambda i: (i, 0, 0), memory_space=pltpu.SMEM),
            pl.BlockSpec(memory_space=pl.ANY),
            pl.BlockSpec((MOE_SUB, LANES), lambda i: (i, 0)),
            pl.BlockSpec((1, LANES, LANES), lambda i: (i, 0, 0)),
            pl.BlockSpec((1, LANES, LANES), lambda i: (i, 0, 0)),
            pl.BlockSpec((1, 8, LANES), lambda i: (i, 0, 0)),
            pl.BlockSpec((MOE_SUB, MOE_SUB), lambda i: (0, 0)),
            pl.BlockSpec((MOE_SUB, d), lambda i: (i, 0)),
            pl.BlockSpec(s13.shape, lambda i: (0, 0)),
            pl.BlockSpec(s2.shape, lambda i: (0, 0)),
            pl.BlockSpec((1, MOE_SUB, d), lambda i: (i // per_b, i % per_b, 0)),
            pl.BlockSpec((1, 1, d), lambda i: (i // per_b, 0, 0)),
        ],
        out_specs=pl.BlockSpec((1, MOE_SUB, d), lambda i: (i // per_b, i % per_b, 0)),
        out_shape=jax.ShapeDtypeStruct((b, s, d), F32),
        scratch_shapes=[
            pltpu.VMEM((ybuf_rows, d), BF16),
            pltpu.VMEM((MOE_SUB, d), F32),
            pltpu.SemaphoreType.DMA((1,)),
        ],
        compiler_params=_params("arbitrary"),
        name="moe_combine",
    )(tabs["tab"], y, wtT, tabs["sub_tabs"][0], tabs["sub_tabs"][1], tabs["q_row8"], sl, h, s13,
      s2, x, g2)


def _final_kernel(x_ref, g_ref, o_ref):
    x = x_ref[0]
    ms = jnp.mean(x * x, axis=-1, keepdims=True)
    o_ref[0] = x * lax.rsqrt(ms + EPS) * g_ref[...]


def _final_norm(x, g, tm):
    b, s, d = x.shape
    return pl.pallas_call(
        _final_kernel,
        grid=(b, s // tm),
        in_specs=[
            pl.BlockSpec((1, tm, d), lambda bi, i: (bi, i, 0)),
            pl.BlockSpec((1, d), lambda bi, i: (0, 0)),
        ],
        out_specs=pl.BlockSpec((1, tm, d), lambda bi, i: (bi, i, 0)),
        out_shape=jax.ShapeDtypeStruct((b, s, d), F32),
        compiler_params=_params("parallel", "parallel"),
        name="final_norm",
    )(x, g.reshape(1, d))


def _tile(s, want):
    t = min(s, want)
    assert s % t == 0, (s, t)
    return t


def kernel(x, c, ada_w, ada_b, norm1_g, norm2_g, ev_w_in, ev_dw_w, ev_dw_b, ev_ln_g, ev_ln_b,
           ev_w_out, od_w_in, od_w_grp, od_scale, od_w_out, moe_w_router, moe_bias, moe_w1, moe_w3,
           moe_w2, sh_w1, sh_w3, sh_w2, final_g):
    b, s, d = x.shape
    depth = ada_w.shape[0]
    tm = _tile(s, 512)
    n_exp = moe_bias.shape[1]
    n_sub = b * s // MOE_SUB
    assert s % MOE_SUB == 0 and _max_tiles(n_exp) < TAB_W, (s, n_exp)

    mod = _ada(c, ada_w, ada_b)
    for l in range(depth):
        sh1, sc1, g1, sh2, sc2, g2 = [m.reshape(b, 1, d) for m in jnp.split(mod[l], 6, axis=-1)]
        i = l // 2
        if l % 2 == 0:
            qkv, u = _even_in(x, norm1_g[l], sh1, sc1, ev_w_in[i].astype(BF16), tm)
            att = _sb_attention(qkv, _tile(s, 256), SB_HEADS)
            x = _even_post(x, att, u, ev_dw_w[i], ev_dw_b[i], ev_ln_g[i], ev_ln_b[i],
                           ev_w_out[i].astype(BF16), g1, tm)
        else:
            u = _odd_in(x, norm1_g[l], sh1, sc1, od_w_in[i].astype(BF16), tm)
            x = _odd_post(x, u, od_w_grp[i].astype(BF16), od_scale[i], od_w_out[i].astype(BF16),
                          g1, _tile(s, 256))
        wr = moe_w_router[l].T
        wr_hi = wr.astype(BF16)
        wr_lo = (wr - wr_hi.astype(F32)).astype(BF16)
        h, wt, cnt = _route(x, norm2_g[l], sh2, sc2, jnp.concatenate([wr_hi, wr_lo], axis=0),
                            moe_bias[l], tm)
        tabs = _dispatch_tables(cnt, n_sub, n_exp)
        xg = _dispatch(h, wt, tabs, n_exp)
        w13 = jnp.concatenate([moe_w1[l], moe_w3[l]], axis=-1).astype(BF16)
        y = _experts(xg, w13, moe_w2[l].astype(BF16), tabs)
        s13 = jnp.concatenate([sh_w1[l], sh_w3[l]], axis=-1).astype(BF16)
        x = _combine(y, wt, tabs, h, s13, sh_w2[l].astype(BF16), x, g2)
    return _final_norm(x, final_g, tm)
```

```python
import functools

import jax
import jax.numpy as jnp
from jax import lax
from jax.experimental import pallas as pl
from jax.experimental.pallas import tpu as pltpu

F32 = jnp.float32
BF16 = jnp.bfloat16

EPS = 1e-6
SB_HEADS = 8
SB_HEAD_DIM = 64
SB_WIDTH = SB_HEADS * SB_HEAD_DIM
POOL_WINDOWS = (2, 4, 8, 16)
N_GROUPS = 8
TOPK_GROUPS = 4
TOP_K = 8
ROUTED_SCALE = 2.5

VMEM_LIMIT_BYTES = 56 * 1024 * 1024
LANES = 128


def _params(*semantics):
    return pltpu.CompilerParams(dimension_semantics=semantics, vmem_limit_bytes=VMEM_LIMIT_BYTES)


def _sigmoid(x):
    return 1.0 / (1.0 + jnp.exp(-x))


def _norm_mod(x, g, shift, scale):
    ms = jnp.mean(x * x, axis=-1, keepdims=True)
    y = x * lax.rsqrt(ms + EPS) * g
    return y * (1.0 + scale) + shift


def _ada_kernel(c_ref, w_ref, b_ref, o_ref):
    c = c_ref[...]
    cond = c * _sigmoid(c)
    o_ref[0] = jnp.dot(cond, w_ref[0], precision=lax.Precision.HIGHEST,
                       preferred_element_type=F32) + b_ref[0]


def _ada(c, ada_w, ada_b):
    depth, d, n6 = ada_w.shape
    b = c.shape[0]
    tn = 1536
    return pl.pallas_call(
        _ada_kernel,
        grid=(depth, n6 // tn),
        in_specs=[
            pl.BlockSpec((b, d), lambda l, j: (0, 0)),
            pl.BlockSpec((1, d, tn), lambda l, j: (l, 0, j)),
            pl.BlockSpec((1, 1, tn), lambda l, j: (l, 0, j)),
        ],
        out_specs=pl.BlockSpec((1, b, tn), lambda l, j: (l, 0, j)),
        out_shape=jax.ShapeDtypeStruct((depth, b, n6), F32),
        compiler_params=_params("parallel", "parallel"),
        name="ada_mod",
    )(c, ada_w, ada_b.reshape(depth, 1, n6))


def _even_in_kernel(x_ref, g_ref, sh_ref, sc_ref, w_ref, qkv_ref, u_ref):
    h = _norm_mod(x_ref[0], g_ref[...], sh_ref[0], sc_ref[0]).astype(BF16)
    n_qkv = qkv_ref.shape[-1]
    cw = u_ref.shape[-1]
    for j in range(n_qkv // cw):
        qkv_ref[0, :, j * cw:(j + 1) * cw] = jnp.dot(
            h, w_ref[:, j * cw:(j + 1) * cw], preferred_element_type=F32).astype(BF16)
    val = jnp.dot(h, w_ref[:, n_qkv:n_qkv + cw], preferred_element_type=F32)
    gate = jnp.dot(h, w_ref[:, n_qkv + cw:], preferred_element_type=F32)
    u_ref[0] = val * _sigmoid(gate)


def _even_in(x, g, shift, scale, w_in, tm):
    b, s, d = x.shape
    n_qkv = 3 * SB_WIDTH
    cw = (w_in.shape[1] - n_qkv) // 2
    return pl.pallas_call(
        _even_in_kernel,
        grid=(b, s // tm),
        in_specs=[
            pl.BlockSpec((1, tm, d), lambda bi, i: (bi, i, 0)),
            pl.BlockSpec((1, d), lambda bi, i: (0, 0)),
            pl.BlockSpec((1, 1, d), lambda bi, i: (bi, 0, 0)),
            pl.BlockSpec((1, 1, d), lambda bi, i: (bi, 0, 0)),
            pl.BlockSpec(w_in.shape, lambda bi, i: (0, 0)),
        ],
        out_specs=[
            pl.BlockSpec((1, tm, n_qkv), lambda bi, i: (bi, i, 0)),
            pl.BlockSpec((1, tm, cw), lambda bi, i: (bi, i, 0)),
        ],
        out_shape=[
            jax.ShapeDtypeStruct((b, s, n_qkv), BF16),
            jax.ShapeDtypeStruct((b, s, cw), F32),
        ],
        compiler_params=_params("parallel", "parallel"),
        name="even_in",
    )(x, g.reshape(1, d), shift, scale, w_in)


SB_UNDERFLOW_MASS = 110.0


def _sb_kernel(qT_ref, k_ref, vT_ref, tri_ref, o_ref, *scratch, t):
    qi = pl.program_id(2)
    nh = qT_ref.shape[1]
    acc_refs, r_refs = scratch[:nh], scratch[nh:]
    tri = tri_ref[...]

    def blocks(j, valid):
        heads = range(nh)
        zs = [jnp.dot(k_ref[0, hd, j], qT_ref[0, hd], preferred_element_type=F32) for hd in heads]
        sps, ds = [], []
        for zT in zs:
            sp = jnp.maximum(zT, 0.0) + jnp.log(1.0 + jnp.exp(-jnp.abs(zT)))
            if valid is not None:
                sp = jnp.where(valid, sp, 0.0)
            sps.append(sp)
            ds.append(zT - sp)
        laters = [jnp.dot(tri, sp.astype(BF16), preferred_element_type=F32) for sp in sps]
        r_min = None
        for hd in heads:
            r_prev = r_refs[hd][...]
            a = jnp.exp(ds[hd] - laters[hd] - r_prev)
            if valid is not None:
                a = jnp.where(valid, a, 0.0)
            acc_refs[hd][...] += jnp.dot(vT_ref[0, hd, j], a.astype(BF16),
                                         preferred_element_type=F32)
            r_new = r_prev + laters[hd][0:1, :] + sps[hd][0:1, :]
            r_refs[hd][...] = r_new
            m = jnp.min(r_new)
            r_min = m if r_min is None else jnp.minimum(r_min, m)
        return r_min

    row = lax.broadcasted_iota(jnp.int32, (t, t), 0)
    col = lax.broadcasted_iota(jnp.int32, (t, t), 1)
    for hd in range(nh):
        acc_refs[hd][...] = jnp.zeros(acc_refs[hd].shape, F32)
        r_refs[hd][...] = jnp.zeros(r_refs[hd].shape, F32)
    r_min = blocks(qi, row < col)

    def cond(carry):
        j, r_min = carry
        return jnp.logical_and(j >= 0, r_min < SB_UNDERFLOW_MASS)

    def body(carry):
        j, _ = carry
        return j - 1, blocks(j, None)

    lax.while_loop(cond, body, (qi - 1, r_min))
    for hd in range(nh):
        o_ref[0, hd] = acc_refs[hd][...].astype(o_ref.dtype)


def _sb_attention(qkv, t, nh):
    b, s, _ = qkv.shape
    h, dh = SB_HEADS, SB_HEAD_DIM
    nkb = s // t
    q = qkv[..., :SB_WIDTH].reshape(b, s, h, dh) * (dh ** -0.5)
    k = qkv[..., SB_WIDTH:2 * SB_WIDTH].reshape(b, s, h, dh)
    v = qkv[..., 2 * SB_WIDTH:].reshape(b, s, h, dh)
    pad = LANES - dh
    qT = jnp.pad(q.transpose(0, 2, 3, 1), ((0, 0), (0, 0), (0, pad), (0, 0)))
    kb = jnp.pad(k.transpose(0, 2, 1, 3), ((0, 0), (0, 0), (0, 0), (0, pad)))
    kb = kb.reshape(b, h, nkb, t, LANES)
    vT = v.transpose(0, 2, 3, 1).reshape(b, h, dh, nkb, t).transpose(0, 1, 3, 2, 4)
    tri = (jnp.arange(t)[None, :] > jnp.arange(t)[:, None]).astype(BF16)
    oT = pl.pallas_call(
        functools.partial(_sb_kernel, t=t),
        grid=(b, h // nh, s // t),
        in_specs=[
            pl.BlockSpec((1, nh, LANES, t), lambda bi, hi, qi: (bi, hi, 0, qi)),
            pl.BlockSpec((1, nh, nkb, t, LANES), lambda bi, hi, qi: (bi, hi, 0, 0, 0)),
            pl.BlockSpec((1, nh, nkb, dh, t), lambda bi, hi, qi: (bi, hi, 0, 0, 0)),
            pl.BlockSpec((t, t), lambda bi, hi, qi: (0, 0)),
        ],
        out_specs=pl.BlockSpec((1, nh, dh, t), lambda bi, hi, qi: (bi, hi, 0, qi)),
        out_shape=jax.ShapeDtypeStruct((b, h, dh, s), BF16),
        scratch_shapes=[pltpu.VMEM((dh, t), F32)] * nh + [pltpu.VMEM((1, t), F32)] * nh,
        compiler_params=_params("parallel", "parallel", "arbitrary"),
        name="sb_attention",
    )(qT, kb, vT, tri)
    return oT.transpose(0, 3, 1, 2).reshape(b, s, SB_WIDTH)


CONV_HALO = 32


def _even_post_kernel(x_ref, att_ref, ucur_ref, uhalo_ref, dww_ref, dwb_ref, lng_ref, lnb_ref,
                      wout_ref, g1_ref, o_ref, ext_ref, *, tm, taps):
    i = pl.program_id(1)
    ext_ref[0:CONV_HALO, :] = jnp.where(i > 0, uhalo_ref[0], 0.0)
    ext_ref[CONV_HALO:CONV_HALO + tm, :] = ucur_ref[0]
    base = CONV_HALO - (taps - 1)
    acc = ext_ref[base:base + tm, :] * dww_ref[0:1, :]
    for k in range(1, taps):
        acc = acc + ext_ref[base + k:base + k + tm, :] * dww_ref[k:k + 1, :]
    conv = acc + dwb_ref[...]
    mu = jnp.mean(conv, axis=-1, keepdims=True)
    cen = conv - mu
    var = jnp.mean(cen * cen, axis=-1, keepdims=True)
    y = cen * lax.rsqrt(var + EPS) * lng_ref[...] + lnb_ref[...]
    c = (y * _sigmoid(y)).astype(BF16)
    cat = jnp.concatenate([att_ref[0], c], axis=-1)
    out = jnp.dot(cat, wout_ref[...], preferred_element_type=F32)
    o_ref[0] = x_ref[0] + g1_ref[0] * out


def _even_post(x, att, u, dw_w, dw_b, ln_g, ln_b, w_out, g1, tm):
    b, s, d = x.shape
    cw = u.shape[-1]
    taps = dw_w.shape[0]
    hb = tm // CONV_HALO
    return pl.pallas_call(
        functools.partial(_even_post_kernel, tm=tm, taps=taps),
        grid=(b, s // tm),
        in_specs=[
            pl.BlockSpec((1, tm, d), lambda bi, i: (bi, i, 0)),
            pl.BlockSpec((1, tm, att.shape[-1]), lambda bi, i: (bi, i, 0)),
            pl.BlockSpec((1, tm, cw), lambda bi, i: (bi, i, 0)),
            pl.BlockSpec((1, CONV_HALO, cw), lambda bi, i: (bi, jnp.maximum(i * hb - 1, 0), 0)),
            pl.BlockSpec((taps, cw), lambda bi, i: (0, 0)),
            pl.BlockSpec((1, cw), lambda bi, i: (0, 0)),
            pl.BlockSpec((1, cw), lambda bi, i: (0, 0)),
            pl.BlockSpec((1, cw), lambda bi, i: (0, 0)),
            pl.BlockSpec(w_out.shape, lambda bi, i: (0, 0)),
            pl.BlockSpec((1, 1, d), lambda bi, i: (bi, 0, 0)),
        ],
        out_specs=pl.BlockSpec((1, tm, d), lambda bi, i: (bi, i, 0)),
        out_shape=jax.ShapeDtypeStruct((b, s, d), F32),
        scratch_shapes=[pltpu.VMEM((CONV_HALO + tm, cw), F32)],
        compiler_params=_params("parallel", "parallel"),
        name="even_post",
    )(x, att, u, u, dw_w.reshape(taps, cw), dw_b.reshape(1, cw), ln_g.reshape(1, cw),
      ln_b.reshape(1, cw), w_out, g1)


def _odd_in_kernel(x_ref, g_ref, sh_ref, sc_ref, w_ref, u_ref):
    h = _norm_mod(x_ref[0], g_ref[...], sh_ref[0], sc_ref[0]).astype(BF16)
    u_ref[0] = jnp.dot(h, w_ref[...], preferred_element_type=F32)


def _odd_in(x, g, shift, scale, w_in, tm):
    b, s, d = x.shape
    return pl.pallas_call(
        _odd_in_kernel,
        grid=(b, s // tm),
        in_specs=[
            pl.BlockSpec((1, tm, d), lambda bi, i: (bi, i, 0)),
            pl.BlockSpec((1, d), lambda bi, i: (0, 0)),
            pl.BlockSpec((1, 1, d), lambda bi, i: (bi, 0, 0)),
            pl.BlockSpec((1, 1, d), lambda bi, i: (bi, 0, 0)),
            pl.BlockSpec(w_in.shape, lambda bi, i: (0, 0)),
        ],
        out_specs=pl.BlockSpec((1, tm, w_in.shape[1]), lambda bi, i: (bi, i, 0)),
        out_shape=jax.ShapeDtypeStruct((b, s, w_in.shape[1]), F32),
        compiler_params=_params("parallel", "parallel"),
        name="odd_in",
    )(x, g.reshape(1, d), shift, scale, w_in)


POOL_HALO = 16
POOL_PAD = 8


def _odd_post_kernel(x_ref, ucur_ref, uhalo_ref, wgrp_ref, scale_ref, wout_ref, g1_ref, o_ref,
                     e_ref, s1_ref, s2_ref, s4_ref, *, tm):
    i = pl.program_id(1)
    d = e_ref.shape[1]
    gw = d // len(POOL_WINDOWS)
    ext = POOL_HALO + tm
    lo = POOL_PAD
    zeros_pad = jnp.zeros((POOL_PAD, d), F32)
    e_ref[0:lo, :] = zeros_pad
    e_ref[lo:lo + POOL_HALO, :] = jnp.where(i > 0, uhalo_ref[0], 0.0)
    e_ref[lo + POOL_HALO:lo + ext, :] = ucur_ref[0]
    s1_ref[0:lo, :] = zeros_pad
    s1_ref[lo:lo + ext, :] = e_ref[lo:lo + ext, :] + e_ref[lo - 1:lo - 1 + ext, :]
    s2_ref[0:lo, :] = zeros_pad
    s2_ref[lo:lo + ext, :] = s1_ref[lo:lo + ext, :] + s1_ref[lo - 2:lo - 2 + ext, :]
    s4_ref[0:lo, :] = zeros_pad
    s4_ref[lo:lo + ext, :] = s2_ref[lo:lo + ext, :] + s2_ref[lo - 4:lo - 4 + ext, :]
    t0 = lo + POOL_HALO
    pos = i * tm + lax.broadcasted_iota(jnp.int32, (tm, 1), 0) + 1
    sums = (
        s1_ref[t0:t0 + tm, 0:gw],
        s2_ref[t0:t0 + tm, gw:2 * gw],
        s4_ref[t0:t0 + tm, 2 * gw:3 * gw],
        s4_ref[t0:t0 + tm, 3 * gw:4 * gw] + s4_ref[t0 - 8:t0 - 8 + tm, 3 * gw:4 * gw],
    )
    outs = []
    for gi, w in enumerate(POOL_WINDOWS):
        cnt = jnp.minimum(pos, w).astype(F32)
        ug = e_ref[t0:t0 + tm, gi * gw:(gi + 1) * gw]
        pooled = sums[gi] / cnt - ug
        outs.append(jnp.dot(pooled.astype(BF16), wgrp_ref[gi], preferred_element_type=F32))
    y = (jnp.concatenate(outs, axis=-1) * scale_ref[...]).astype(BF16)
    out = jnp.dot(y, wout_ref[...], preferred_element_type=F32)
    o_ref[0] = x_ref[0] + g1_ref[0] * out


def _odd_post(x, u, w_grp, ch_scale, w_out, g1, tm):
    b, s, d = x.shape
    hb = tm // POOL_HALO
    rows = POOL_PAD + POOL_HALO + tm
    return pl.pallas_call(
        functools.partial(_odd_post_kernel, tm=tm),
        grid=(b, s // tm),
        in_specs=[
            pl.BlockSpec((1, tm, d), lambda bi, i: (bi, i, 0)),
            pl.BlockSpec((1, tm, d), lambda bi, i: (bi, i, 0)),
            pl.BlockSpec((1, POOL_HALO, d), lambda bi, i: (bi, jnp.maximum(i * hb - 1, 0), 0)),
            pl.BlockSpec(w_grp.shape, lambda bi, i: (0, 0, 0)),
            pl.BlockSpec((1, d), lambda bi, i: (0, 0)),
            pl.BlockSpec(w_out.shape, lambda bi, i: (0, 0)),
            pl.BlockSpec((1, 1, d), lambda bi, i: (bi, 0, 0)),
        ],
        out_specs=pl.BlockSpec((1, tm, d), lambda bi, i: (bi, i, 0)),
        out_shape=jax.ShapeDtypeStruct((b, s, d), F32),
        scratch_shapes=[pltpu.VMEM((rows, d), F32)] * 4,
        compiler_params=_params("parallel", "parallel"),
        name="odd_post",
    )(x, u, u, w_grp, ch_scale.reshape(1, d), w_out, g1)


def _route_kernel(x_ref, g_ref, sh_ref, sc_ref, wr_ref, bias_ref, h_ref, wt_ref, cnt_ref, *, n_exp):
    h = _norm_mod(x_ref[0], g_ref[...], sh_ref[0], sc_ref[0])
    h_hi = h.astype(BF16)
    h_lo = (h - h_hi.astype(F32)).astype(BF16)
    h_ref[...] = h_hi
    nt = (((1,), (1,)), ((), ()))
    both = lax.dot_general(wr_ref[...], h_hi, nt, preferred_element_type=F32)
    logits = (both[:n_exp] + both[n_exp:]
              + lax.dot_general(wr_ref[0:n_exp, :], h_lo, nt, preferred_element_type=F32))
    scores = _sigmoid(logits)
    biased = scores + bias_ref[...]
    per = n_exp // N_GROUPS
    tm = scores.shape[1]
    sub = lax.broadcasted_iota(jnp.int32, (per, tm), 0)
    slabs = [biased[g * per:(g + 1) * per, :] for g in range(N_GROUPS)]
    gs = []
    for sl in slabs:
        m1 = jnp.max(sl, axis=0, keepdims=True)
        first = jnp.min(jnp.where(sl == m1, sub, per), axis=0, keepdims=True)
        m2 = jnp.max(jnp.where(sub == first, -jnp.inf, sl), axis=0, keepdims=True)
        gs.append(m1 + m2)
    masked = []
    for g in range(N_GROUPS):
        rank = jnp.zeros((1, tm), jnp.int32)
        for g2 in range(N_GROUPS):
            if g2 < g:
                rank = rank + (gs[g2] >= gs[g]).astype(jnp.int32)
            elif g2 > g:
                rank = rank + (gs[g2] > gs[g]).astype(jnp.int32)
        masked.append(jnp.where(rank < TOPK_GROUPS, slabs[g], -jnp.inf))
    ranks = [jnp.zeros((per, tm), jnp.int32) for _ in range(N_GROUPS)]
    for g2 in range(N_GROUPS):
        for i2 in range(per):
            r = masked[g2][i2:i2 + 1, :]
            for g in range(N_GROUPS):
                if g2 < g:
                    beats = r >= masked[g]
                elif g2 > g:
                    beats = r > masked[g]
                else:
                    beats = (r > masked[g]) | ((r == masked[g]) & (sub > i2))
                ranks[g] = ranks[g] + beats.astype(jnp.int32)
    picked = [jnp.where(ranks[g] < TOP_K, scores[g * per:(g + 1) * per, :], 0.0)
              for g in range(N_GROUPS)]
    tot = picked[0]
    for g in range(1, N_GROUPS):
        tot = tot + picked[g]
    denom = jnp.sum(tot, axis=0, keepdims=True)
    for g in range(N_GROUPS):
        wt_ref[g * per:(g + 1) * per, :] = picked[g] / denom * ROUTED_SCALE
    ones = jnp.ones((MOE_SUB, LANES), BF16)
    for k in range(tm // MOE_SUB):
        sel = jnp.where(wt_ref[:, k * MOE_SUB:(k + 1) * MOE_SUB] > 0.0, 1.0, 0.0).astype(BF16)
        cnt_ref[:, k * LANES:(k + 1) * LANES] = jnp.dot(sel, ones, preferred_element_type=F32)


def _route(x, g, shift, scale, wr2, bias, tm):
    b, s, d = x.shape
    n = b * s
    n_exp = bias.shape[0]
    per_b = s // tm
    cw = tm // MOE_SUB * LANES
    return pl.pallas_call(
        functools.partial(_route_kernel, n_exp=n_exp),
        grid=(b, per_b),
        in_specs=[
            pl.BlockSpec((1, tm, d), lambda bi, i: (bi, i, 0)),
            pl.BlockSpec((1, d), lambda bi, i: (0, 0)),
            pl.BlockSpec((1, 1, d), lambda bi, i: (bi, 0, 0)),
            pl.BlockSpec((1, 1, d), lambda bi, i: (bi, 0, 0)),
            pl.BlockSpec(wr2.shape, lambda bi, i: (0, 0)),
            pl.BlockSpec((n_exp, 1), lambda bi, i: (0, 0)),
        ],
        out_specs=[
            pl.BlockSpec((tm, d), lambda bi, i: (bi * per_b + i, 0)),
            pl.BlockSpec((n_exp, tm), lambda bi, i: (0, bi * per_b + i)),
            pl.BlockSpec((n_exp, cw), lambda bi, i: (0, bi * per_b + i)),
        ],
        out_shape=[
            jax.ShapeDtypeStruct((n, d), BF16),
            jax.ShapeDtypeStruct((n_exp, n), F32),
            jax.ShapeDtypeStruct((n_exp, n // MOE_SUB * LANES), F32),
        ],
        compiler_params=_params("parallel", "parallel"),
        name="moe_route",
    )(x, g.reshape(1, d), shift, scale, wr2, bias.reshape(n_exp, 1))


MOE_SUB = 256
MOE_TILE = 16
MOE_ROWBLK = 512
MOE_CHUNK = 512
TILES_PER_CHUNK = MOE_CHUNK // MOE_TILE
TILES_PER_BLK = MOE_ROWBLK // MOE_TILE
TAB_W = 256


def _max_tiles(n_exp):
    worst = (TOP_K * MOE_SUB + n_exp * (MOE_TILE - 1)) // MOE_TILE
    return -(-worst // TILES_PER_CHUNK) * TILES_PER_CHUNK


def _dispatch_tables(cnt, n_sub, n_exp):
    i32 = jnp.int32
    nt_max = _max_tiles(n_exp)
    blk = MOE_ROWBLK // MOE_TILE
    n = cnt[:, ::LANES].T.astype(i32)
    np16 = (n + (MOE_TILE - 1)) // MOE_TILE
    q_end = jnp.cumsum(np16, axis=1)
    q_off = q_end - np16
    n_tiles = q_end[:, -1]
    g_off = jnp.cumsum(np16, axis=0) - np16
    tot = jnp.sum(np16, axis=0)
    tot_r = (tot + (blk - 1)) // blk * blk
    bend = jnp.cumsum(tot_r)
    base = bend - tot_r
    j = jnp.arange(nt_max, dtype=i32)[None, :, None]
    owns = (j >= q_off[:, None, :]) & (j < q_end[:, None, :])
    shift = (base[None, :] + g_off - q_off)[:, None, :]
    dest = jnp.sum(jnp.where(owns, shift + j, 0), axis=-1)
    tab = jnp.concatenate(
        [n_tiles[:, None], dest, jnp.zeros((n_sub, TAB_W - 1 - nt_max), i32)], axis=1)
    n_blocks = (TOP_K * n_sub * MOE_SUB + n_sub * n_exp * (MOE_TILE - 1)
                + n_exp * (MOE_ROWBLK - 1)) // MOE_ROWBLK + 1
    n_used = (bend[-1] // blk).astype(i32).reshape(1)
    gaps = jnp.stack([base + tot, tot_r - tot, jnp.broadcast_to(n_used, (n_exp,))]).astype(i32)
    b_idx = jnp.minimum(jnp.arange(n_blocks, dtype=i32), n_used[0] - 1)
    block_e = jnp.minimum(jnp.sum(b_idx[:, None] * blk >= bend[None, :], axis=1), n_exp - 1)
    row_lo = (q_off * MOE_TILE).astype(F32)
    row_hi = (q_end * MOE_TILE).astype(F32)
    pad_l = ((0, 0), (0, LANES - n_exp))
    lane_tabs = (jnp.pad(row_lo, pad_l).reshape(n_sub, 1, LANES),
                 jnp.pad(row_hi, pad_l).reshape(n_sub, 1, LANES))
    pad_s = ((0, 0), (0, LANES - n_exp), (0, 0))
    sub_tabs = (jnp.pad(jnp.broadcast_to(row_lo[:, :, None], (n_sub, n_exp, LANES)), pad_s),
                jnp.pad(jnp.broadcast_to(row_hi[:, :, None], (n_sub, n_exp, LANES)), pad_s))
    q_row = jnp.pad(q_off.astype(F32), pad_l)
    q_row8 = jnp.broadcast_to(q_row[:, None, :], (n_sub, 8, LANES)).astype(BF16)
    q_col = jnp.broadcast_to(q_off.astype(F32)[:, :, None], (n_sub, n_exp, LANES)).astype(BF16)
    return dict(tab=tab.reshape(n_sub, 1, TAB_W), gaps=gaps, n_blocks=n_blocks, n_used=n_used,
                block_e=block_e.astype(i32), lane_tabs=lane_tabs, sub_tabs=sub_tabs,
                q_row8=q_row8, q_col=q_col)


def _tile_rows(tile):
    if isinstance(tile, int):
        return pl.ds(tile * MOE_TILE, MOE_TILE)
    return pl.ds(pl.multiple_of(tile * MOE_TILE, MOE_TILE), MOE_TILE)


def _tile_copy(src_ref, src_tile, dst_ref, dst_tile, sem):
    return pltpu.make_async_copy(src_ref.at[_tile_rows(src_tile)], dst_ref.at[_tile_rows(dst_tile)],
                                 sem)


def _dispatch_kernel(tab_ref, gap_ref, h_ref, wt_ref, lo_ref, hi_ref, qcol_ref, su_ref, xg_ref,
                     stage_ref, zero_ref, sem, *, n_exp, d, n_blocks):
    s = pl.program_id(0)
    spare_tile0 = n_blocks * TILES_PER_BLK
    n_blk_total = xg_ref.shape[0] // MOE_ROWBLK
    n_tiles = tab_ref[0, 0, 0]
    n_chunks = (n_tiles + TILES_PER_CHUNK - 1) // TILES_PER_CHUNK
    wt = wt_ref[...]
    sel = wt > 0.0
    pos = jnp.dot(jnp.where(sel, 1.0, 0.0).astype(BF16), su_ref[...], preferred_element_type=F32)
    posm = jnp.where(sel, pos, -1.0).astype(BF16)
    wt_hi = wt.astype(BF16)
    wt_lo = (wt - wt_hi.astype(F32)).astype(BF16)
    top = jnp.concatenate([posm, qcol_ref[0], wt_hi, wt_lo], axis=1)
    rhs = jnp.concatenate([top, jnp.zeros((LANES - n_exp, top.shape[1]), BF16)], axis=0)
    h = h_ref[...]
    row_lo = lo_ref[0]
    row_hi = hi_ref[0]
    lane = lax.broadcasted_iota(jnp.int32, (MOE_CHUNK, LANES), 1)
    sub = MOE_SUB

    def chunk(c, carry):
        r = (c * MOE_CHUNK + lax.broadcasted_iota(jnp.int32, (MOE_CHUNK, LANES), 0)).astype(F32)
        owner = jnp.where((r >= row_lo) & (r < row_hi), 1.0, 0.0).astype(BF16)
        g = jnp.dot(owner, rhs, preferred_element_type=F32)
        j = r - MOE_TILE * g[:, sub:sub + LANES]
        hit = g[:, 0:sub] == jnp.concatenate([j] * (sub // LANES), axis=1)
        rows = jnp.dot(jnp.where(hit, 1.0, 0.0).astype(BF16), h, preferred_element_type=F32)
        wg = g[:, sub + LANES:2 * sub + LANES] + g[:, 2 * sub + LANES:]
        w = jnp.sum(jnp.where(hit, wg, 0.0), axis=1, keepdims=True)
        w_hi = w.astype(BF16).astype(F32)
        w_lo = (w - w_hi).astype(BF16).astype(F32)
        r0 = pl.multiple_of(c * MOE_CHUNK, MOE_CHUNK)
        stage_ref[pl.ds(r0, MOE_CHUNK), 0:d] = rows.astype(BF16)
        stage_ref[pl.ds(r0, MOE_CHUNK), d:] = jnp.where(lane < LANES // 2, w_hi, w_lo).astype(BF16)
        for jt in range(TILES_PER_CHUNK):
            t = c * TILES_PER_CHUNK + jt
            dst = jnp.where(t < n_tiles, tab_ref[0, 0, 1 + t], spare_tile0 + t)
            _tile_copy(stage_ref, t, xg_ref, dst, sem.at[0]).start()
        return carry

    lax.fori_loop(0, n_chunks, chunk, 0)

    def drain(c, carry):
        for jt in range(TILES_PER_CHUNK):
            _tile_copy(stage_ref, 0, xg_ref, 0, sem.at[0]).wait()
        return carry

    lax.fori_loop(0, n_chunks, drain, 0)

    @pl.when(s == pl.num_programs(0) - 1)
    def _():
        zero_ref[...] = jnp.zeros_like(zero_ref)

        def gap_copy(dst_tile):
            return _tile_copy(zero_ref, 0, xg_ref, dst_tile, sem.at[1])

        def blk_copy(blk):
            r0 = blk * MOE_ROWBLK
            if not isinstance(blk, int):
                r0 = pl.multiple_of(r0, MOE_ROWBLK)
            return pltpu.make_async_copy(zero_ref, xg_ref.at[pl.ds(r0, MOE_ROWBLK)], sem.at[2])

        def per_expert(e, carry):
            def one(k, c2):
                gap_copy(gap_ref[0, e] + k).start()
                return c2
            return lax.fori_loop(0, gap_ref[1, e], one, carry)

        lax.fori_loop(0, n_exp, per_expert, 0)
        n_used = gap_ref[2, 0]

        def fill(blk, carry):
            blk_copy(blk).start()
            return carry

        lax.fori_loop(n_used, n_blk_total, fill, 0)

        def per_expert_wait(e, carry):
            def one(k, c2):
                gap_copy(0).wait()
                return c2
            return lax.fori_loop(0, gap_ref[1, e], one, carry)

        lax.fori_loop(0, n_exp, per_expert_wait, 0)

        def fill_wait(blk, carry):
            blk_copy(0).wait()
            return carry

        lax.fori_loop(n_used, n_blk_total, fill_wait, 0)


def _dispatch(h, wt, tabs, n_exp):
    n, d = h.shape
    n_sub = n // MOE_SUB
    xw = d + LANES
    stage_rows = _max_tiles(n_exp) * MOE_TILE
    su = (jnp.arange(MOE_SUB)[:, None] < jnp.arange(MOE_SUB)[None, :]).astype(BF16)
    smem = pltpu.SMEM
    total_rows = tabs["n_blocks"] * MOE_ROWBLK + -(-stage_rows // MOE_ROWBLK) * MOE_ROWBLK
    return pl.pallas_call(
        functools.partial(_dispatch_kernel, n_exp=n_exp, d=d, n_blocks=tabs["n_blocks"]),
        grid=(n_sub,),
        in_specs=[
            pl.BlockSpec((1, 1, TAB_W), lambda s: (s, 0, 0), memory_space=smem),
            pl.BlockSpec(memory_space=smem),
            pl.BlockSpec((MOE_SUB, d), lambda s: (s, 0)),
            pl.BlockSpec((n_exp, MOE_SUB), lambda s: (0, s)),
            pl.BlockSpec((1, 1, LANES), lambda s: (s, 0, 0)),
            pl.BlockSpec((1, 1, LANES), lambda s: (s, 0, 0)),
            pl.BlockSpec((1, n_exp, LANES), lambda s: (s, 0, 0)),
            pl.BlockSpec((MOE_SUB, MOE_SUB), lambda s: (0, 0)),
        ],
        out_specs=pl.BlockSpec(memory_space=pl.ANY),
        out_shape=jax.ShapeDtypeStruct((total_rows, xw), BF16),
        scratch_shapes=[
            pltpu.VMEM((stage_rows, xw), BF16),
            pltpu.VMEM((MOE_ROWBLK, xw), BF16),
            pltpu.SemaphoreType.DMA((3,)),
        ],
        compiler_params=_params("arbitrary"),
        name="moe_dispatch",
    )(tabs["tab"], tabs["gaps"], h, wt, tabs["lane_tabs"][0], tabs["lane_tabs"][1], tabs["q_col"],
      su)


def _expert_kernel(be_ref, nu_ref, x_ref, w13_ref, w2_ref, y_ref, *, d, f):
    @pl.when(pl.program_id(0) >= nu_ref[0])
    def _():
        y_ref[...] = jnp.zeros_like(y_ref)

    @pl.when(pl.program_id(0) < nu_ref[0])
    def _():
        x = x_ref[...]
        half = LANES // 2
        w = x[:, d:d + 1].astype(F32) + x[:, d + half:d + half + 1].astype(F32)
        ab = jnp.dot(x[:, 0:d], w13_ref[0], preferred_element_type=F32)
        a = ab[:, 0:f]
        gated = (a * _sigmoid(a)) * ab[:, f:] * w
        y_ref[...] = jnp.dot(gated.astype(BF16), w2_ref[0],
                             preferred_element_type=F32).astype(y_ref.dtype)


def _experts(xg, w13, w2, tabs):
    rows, xw = xg.shape
    n_exp, d, f2 = w13.shape
    f = f2 // 2

    def blk(b, be, nu):
        return jnp.minimum(b, nu[0] - 1)

    return pl.pallas_call(
        functools.partial(_expert_kernel, d=d, f=f),
        grid_spec=pltpu.PrefetchScalarGridSpec(
            num_scalar_prefetch=2,
            grid=(tabs["n_blocks"],),
            in_specs=[
                pl.BlockSpec((MOE_ROWBLK, xw), lambda b, be, nu: (blk(b, be, nu), 0)),
                pl.BlockSpec((1, d, f2), lambda b, be, nu: (be[b], 0, 0)),
                pl.BlockSpec((1, f, d), lambda b, be, nu: (be[b], 0, 0)),
            ],
            out_specs=pl.BlockSpec((MOE_ROWBLK, d), lambda b, be, nu: (b, 0)),
        ),
        out_shape=jax.ShapeDtypeStruct((tabs["n_blocks"] * MOE_ROWBLK, d), BF16),
        compiler_params=_params("arbitrary"),
        name="moe_experts",
    )(tabs["block_e"], tabs["n_used"], xg, w13, w2)


def _combine_kernel(tab_ref, y_ref, wtT_ref, lo_ref, hi_ref, q8_ref, sl_ref, h_ref, s13_ref,
                    s2_ref, x_ref, g2_ref, o_ref, ybuf_ref, acc_ref, sem, *, f):
    n_tiles = tab_ref[0, 0, 0]
    n_chunks = (n_tiles + TILES_PER_CHUNK - 1) // TILES_PER_CHUNK
    d = ybuf_ref.shape[1]

    def fetch(c, carry):
        for jt in range(TILES_PER_CHUNK):
            t = c * TILES_PER_CHUNK + jt
            _tile_copy(y_ref, tab_ref[0, 0, 1 + t], ybuf_ref, t, sem.at[0]).start()
        return carry

    lax.fori_loop(0, n_chunks, fetch, 0)

    ab = jnp.dot(h_ref[...], s13_ref[...], preferred_element_type=F32)
    a = ab[:, 0:f]
    acc_ref[...] = jnp.dot(((a * _sigmoid(a)) * ab[:, f:]).astype(BF16), s2_ref[...],
                           preferred_element_type=F32)

    sel = wtT_ref[...] > 0.0
    pos = jnp.dot(sl_ref[...], jnp.where(sel, 1.0, 0.0).astype(BF16), preferred_element_type=F32)
    posm = jnp.where(sel, pos, -1.0).astype(BF16)
    row_lo = jnp.concatenate([lo_ref[0]] * (MOE_CHUNK // LANES), axis=1)
    row_hi = jnp.concatenate([hi_ref[0]] * (MOE_CHUNK // LANES), axis=1)
    q8 = q8_ref[0]

    def drain(c, carry):
        for jt in range(TILES_PER_CHUNK):
            _tile_copy(y_ref, 0, ybuf_ref, 0, sem.at[0]).wait()
        return carry

    lax.fori_loop(0, n_chunks, drain, 0)

    def chunk(c, carry):
        r = (c * MOE_CHUNK + lax.broadcasted_iota(jnp.int32, (LANES, MOE_CHUNK), 1)).astype(F32)
        owner = jnp.where((r >= row_lo) & (r < row_hi), 1.0, 0.0).astype(BF16)
        g = jnp.dot(posm, owner, preferred_element_type=F32)
        q = jnp.dot(q8, owner, preferred_element_type=F32)
        j = r[0:1, :] - MOE_TILE * q[0:1, :]
        hit = jnp.where(g == j, 1.0, 0.0).astype(BF16)
        r0 = pl.multiple_of(c * MOE_CHUNK, MOE_CHUNK)
        acc_ref[...] += jnp.dot(hit, ybuf_ref[pl.ds(r0, MOE_CHUNK), :], preferred_element_type=F32)
        return carry

    lax.fori_loop(0, n_chunks, chunk, 0)
    o_ref[0] = x_ref[0] + g2_ref[0] * acc_ref[...]


def _combine(y, wt, tabs, h, s13, s2, x, g2):
    b, s, d = x.shape
    n = b * s
    n_exp = wt.shape[0]
    n_sub = n // MOE_SUB
    per_b = s // MOE_SUB
    f = s2.shape[0]
    wtT = jnp.pad(wt.T, ((0, 0), (0, LANES - n_exp)))
    sl = (jnp.arange(MOE_SUB)[None, :] < jnp.arange(MOE_SUB)[:, None]).astype(BF16)
    ybuf_rows = _max_tiles(n_exp) * MOE_TILE
    return pl.pallas_call(
        functools.partial(_combine_kernel, f=f),
        grid=(n_sub,),
        in_specs=[
            pl.BlockSpec((1, 1, TAB_W), lambda i: (i, 0, 0), memory_space=pltpu.SMEM),
            pl.BlockSpec(memory_space=pl.ANY),
            pl.BlockSpec((MOE_SUB, LANES), lambda i: (i, 0)),
            pl.BlockSpec((1, LANES, LANES), lambda i: (i, 0, 0)),
            pl.BlockSpec((1, LANES, LANES), lambda i: (i, 0, 0)),
            pl.BlockSpec((1, 8, LANES), lambda i: (i, 0, 0)),
            pl.BlockSpec((MOE_SUB, MOE_SUB), lambda i: (0, 0)),
            pl.BlockSpec((MOE_SUB, d), lambda i: (i, 0)),
            pl.BlockSpec(s13.shape, lambda i: (0, 0)),
            pl.BlockSpec(s2.shape, lambda i: (0, 0)),
            pl.BlockSpec((1, MOE_SUB, d), lambda i: (i // per_b, i % per_b, 0)),
            pl.BlockSpec((1, 1, d), lambda i: (i // per_b, 0, 0)),
        ],
        out_specs=pl.BlockSpec((1, MOE_SUB, d), lambda i: (i // per_b, i % per_b, 0)),
        out_shape=jax.ShapeDtypeStruct((b, s, d), F32),
        scratch_shapes=[
            pltpu.VMEM((ybuf_rows, d), BF16),
            pltpu.VMEM((MOE_SUB, d), F32),
            pltpu.SemaphoreType.DMA((1,)),
        ],
        compiler_params=_params("arbitrary"),
        name="moe_combine",
    )(tabs["tab"], y, wtT, tabs["sub_tabs"][0], tabs["sub_tabs"][1], tabs["q_row8"], sl, h, s13,
      s2, x, g2)


def _final_kernel(x_ref, g_ref, o_ref):
    x = x_ref[0]
    ms = jnp.mean(x * x, axis=-1, keepdims=True)
    o_ref[0] = x * lax.rsqrt(ms + EPS) * g_ref[...]


def _final_norm(x, g, tm):
    b, s, d = x.shape
    return pl.pallas_call(
        _final_kernel,
        grid=(b, s // tm),
        in_specs=[
            pl.BlockSpec((1, tm, d), lambda bi, i: (bi, i, 0)),
            pl.BlockSpec((1, d), lambda bi, i: (0, 0)),
        ],
        out_specs=pl.BlockSpec((1, tm, d), lambda bi, i: (bi, i, 0)),
        out_shape=jax.ShapeDtypeStruct((b, s, d), F32),
        compiler_params=_params("parallel", "parallel"),
        name="final_norm",
    )(x, g.reshape(1, d))


def _tile(s, want):
    t = min(s, want)
    assert s % t == 0, (s, t)
    return t


def kernel(x, c, ada_w, ada_b, norm1_g, norm2_g, ev_w_in, ev_dw_w, ev_dw_b, ev_ln_g, ev_ln_b,
           ev_w_out, od_w_in, od_w_grp, od_scale, od_w_out, moe_w_router, moe_bias, moe_w1, moe_w3,
           moe_w2, sh_w1, sh_w3, sh_w2, final_g):
    b, s, d = x.shape
    depth = ada_w.shape[0]
    tm = _tile(s, 512)
    n_exp = moe_bias.shape[1]
    n_sub = b * s // MOE_SUB
    assert s % MOE_SUB == 0 and _max_tiles(n_exp) < TAB_W, (s, n_exp)

    mod = _ada(c, ada_w, ada_b)
    for l in range(depth):
        sh1, sc1, g1, sh2, sc2, g2 = [m.reshape(b, 1, d) for m in jnp.split(mod[l], 6, axis=-1)]
        i = l // 2
        if l % 2 == 0:
            qkv, u = _even_in(x, norm1_g[l], sh1, sc1, ev_w_in[i].astype(BF16), tm)
            att = _sb_attention(qkv, _tile(s, 256), SB_HEADS)
            x = _even_post(x, att, u, ev_dw_w[i], ev_dw_b[i], ev_ln_g[i], ev_ln_b[i],
                           ev_w_out[i].astype(BF16), g1, tm)
        else:
            u = _odd_in(x, norm1_g[l], sh1, sc1, od_w_in[i].astype(BF16), tm)
            x = _odd_post(x, u, od_w_grp[i].astype(BF16), od_scale[i], od_w_out[i].astype(BF16),
                          g1, _tile(s, 256))
        wr = moe_w_router[l].T
        wr_hi = wr.astype(BF16)
        wr_lo = (wr - wr_hi.astype(F32)).astype(BF16)
        h, wt, cnt = _route(x, norm2_g[l], sh2, sc2, jnp.concatenate([wr_hi, wr_lo], axis=0),
                            moe_bias[l], tm)
        tabs = _dispatch_tables(cnt, n_sub, n_exp)
        xg = _dispatch(h, wt, tabs, n_exp)
        w13 = jnp.concatenate([moe_w1[l], moe_w3[l]], axis=-1).astype(BF16)
        y = _experts(xg, w13, moe_w2[l].astype(BF16), tabs)
        s13 = jnp.concatenate([sh_w1[l], sh_w3[l]], axis=-1).astype(BF16)
        x = _combine(y, wt, tabs, h, s13, sh_w2[l].astype(BF16), x, g2)
    return _final_norm(x, final_g, tm)
```

```python
import functools

import jax
import jax.numpy as jnp
from jax import lax
from jax.experimental import pallas as pl
from jax.experimental.pallas import tpu as pltpu

F32 = jnp.float32
BF16 = jnp.bfloat16

EPS = 1e-6
SB_HEADS = 8
SB_HEAD_DIM = 64
SB_WIDTH = SB_HEADS * SB_HEAD_DIM
POOL_WINDOWS = (2, 4, 8, 16)
N_GROUPS = 8
TOPK_GROUPS = 4
TOP_K = 8
ROUTED_SCALE = 2.5

VMEM_LIMIT_BYTES = 56 * 1024 * 1024
LANES = 128


def _params(*semantics):
    return pltpu.CompilerParams(dimension_semantics=semantics, vmem_limit_bytes=VMEM_LIMIT_BYTES)


def _sigmoid(x):
    return 1.0 / (1.0 + jnp.exp(-x))


def _norm_mod(x, g, shift, scale):
    ms = jnp.mean(x * x, axis=-1, keepdims=True)
    y = x * lax.rsqrt(ms + EPS) * g
    return y * (1.0 + scale) + shift


def _ada_kernel(c_ref, w_ref, b_ref, o_ref):
    c = c_ref[...]
    cond = c * _sigmoid(c)
    o_ref[0] = jnp.dot(cond, w_ref[0], precision=lax.Precision.HIGHEST,
                       preferred_element_type=F32) + b_ref[0]


def _ada(c, ada_w, ada_b):
    depth, d, n6 = ada_w.shape
    b = c.shape[0]
    tn = 1536
    return pl.pallas_call(
        _ada_kernel,
        grid=(depth, n6 // tn),
        in_specs=[
            pl.BlockSpec((b, d), lambda l, j: (0, 0)),
            pl.BlockSpec((1, d, tn), lambda l, j: (l, 0, j)),
            pl.BlockSpec((1, 1, tn), lambda l, j: (l, 0, j)),
        ],
        out_specs=pl.BlockSpec((1, b, tn), lambda l, j: (l, 0, j)),
        out_shape=jax.ShapeDtypeStruct((depth, b, n6), F32),
        compiler_params=_params("parallel", "parallel"),
        name="ada_mod",
    )(c, ada_w, ada_b.reshape(depth, 1, n6))


def _even_in_kernel(x_ref, g_ref, sh_ref, sc_ref, w_ref, qkv_ref, u_ref):
    h = _norm_mod(x_ref[0], g_ref[...], sh_ref[0], sc_ref[0]).astype(BF16)
    n_qkv = qkv_ref.shape[-1]
    cw = u_ref.shape[-1]
    for j in range(n_qkv // cw):
        qkv_ref[0, :, j * cw:(j + 1) * cw] = jnp.dot(
            h, w_ref[:, j * cw:(j + 1) * cw], preferred_element_type=F32).astype(BF16)
    val = jnp.dot(h, w_ref[:, n_qkv:n_qkv + cw], preferred_element_type=F32)
    gate = jnp.dot(h, w_ref[:, n_qkv + cw:], preferred_element_type=F32)
    u_ref[0] = val * _sigmoid(gate)


def _even_in(x, g, shift, scale, w_in, tm):
    b, s, d = x.shape
    n_qkv = 3 * SB_WIDTH
    cw = (w_in.shape[1] - n_qkv) // 2
    return pl.pallas_call(
        _even_in_kernel,
        grid=(b, s // tm),
        in_specs=[
            pl.BlockSpec((1, tm, d), lambda bi, i: (bi, i, 0)),
            pl.BlockSpec((1, d), lambda bi, i: (0, 0)),
            pl.BlockSpec((1, 1, d), lambda bi, i: (bi, 0, 0)),
            pl.BlockSpec((1, 1, d), lambda bi, i: (bi, 0, 0)),
            pl.BlockSpec(w_in.shape, lambda bi, i: (0, 0)),
        ],
        out_specs=[
            pl.BlockSpec((1, tm, n_qkv), lambda bi, i: (bi, i, 0)),
            pl.BlockSpec((1, tm, cw), lambda bi, i: (bi, i, 0)),
        ],
        out_shape=[
            jax.ShapeDtypeStruct((b, s, n_qkv), BF16),
            jax.ShapeDtypeStruct((b, s, cw), F32),
        ],
        compiler_params=_params("parallel", "parallel"),
        name="even_in",
    )(x, g.reshape(1, d), shift, scale, w_in)


SB_UNDERFLOW_MASS = 110.0


def _sb_kernel(qT_ref, k_ref, vT_ref, tri_ref, o_ref, *scratch, t):
    qi = pl.program_id(2)
    nh = qT_ref.shape[1]
    acc_refs, r_refs = scratch[:nh], scratch[nh:]
    tri = tri_ref[...]

    def blocks(j, valid):
        heads = range(nh)
        zs = [jnp.dot(k_ref[0, hd, j], qT_ref[0, hd], preferred_element_type=F32) for hd in heads]
        sps, ds = [], []
        for zT in zs:
            sp = jnp.maximum(zT, 0.0) + jnp.log(1.0 + jnp.exp(-jnp.abs(zT)))
            if valid is not None:
                sp = jnp.where(valid, sp, 0.0)
            sps.append(sp)
            ds.append(zT - sp)
        laters = [jnp.dot(tri, sp.astype(BF16), preferred_element_type=F32) for sp in sps]
        r_min = None
        for hd in heads:
            r_prev = r_refs[hd][...]
            a = jnp.exp(ds[hd] - laters[hd] - r_prev)
            if valid is not None:
                a = jnp.where(valid, a, 0.0)
            acc_refs[hd][...] += jnp.dot(vT_ref[0, hd, j], a.astype(BF16),
                                         preferred_element_type=F32)
            r_new = r_prev + laters[hd][0:1, :] + sps[hd][0:1, :]
            r_refs[hd][...] = r_new
            m = jnp.min(r_new)
            r_min = m if r_min is None else jnp.minimum(r_min, m)
        return r_min

    row = lax.broadcasted_iota(jnp.int32, (t, t), 0)
    col = lax.broadcasted_iota(jnp.int32, (t, t), 1)
    for hd in range(nh):
        acc_refs[hd][...] = jnp.zeros(acc_refs[hd].shape, F32)
        r_refs[hd][...] = jnp.zeros(r_refs[hd].shape, F32)
    r_min = blocks(qi, row < col)

    def cond(carry):
        j, r_min = carry
        return jnp.logical_and(j >= 0, r_min < SB_UNDERFLOW_MASS)

    def body(carry):
        j, _ = carry
        return j - 1, blocks(j, None)

    lax.while_loop(cond, body, (qi - 1, r_min))
    for hd in range(nh):
        o_ref[0, hd] = acc_refs[hd][...].astype(o_ref.dtype)


def _sb_attention(qkv, t, nh):
    b, s, _ = qkv.shape
    h, dh = SB_HEADS, SB_HEAD_DIM
    nkb = s // t
    q = qkv[..., :SB_WIDTH].reshape(b, s, h, dh) * (dh ** -0.5)
    k = qkv[..., SB_WIDTH:2 * SB_WIDTH].reshape(b, s, h, dh)
    v = qkv[..., 2 * SB_WIDTH:].reshape(b, s, h, dh)
    qT = q.transpose(0, 2, 3, 1)
    kb = k.transpose(0, 2, 1, 3).reshape(b, h, nkb, t, dh)
    vT = v.transpose(0, 2, 3, 1).reshape(b, h, dh, nkb, t).transpose(0, 1, 3, 2, 4)
    tri = (jnp.arange(t)[None, :] > jnp.arange(t)[:, None]).astype(BF16)
    oT = pl.pallas_call(
        functools.partial(_sb_kernel, t=t),
        grid=(b, h // nh, s // t),
        in_specs=[
            pl.BlockSpec((1, nh, dh, t), lambda bi, hi, qi: (bi, hi, 0, qi)),
            pl.BlockSpec((1, nh, nkb, t, dh), lambda bi, hi, qi: (bi, hi, 0, 0, 0)),
            pl.BlockSpec((1, nh, nkb, dh, t), lambda bi, hi, qi: (bi, hi, 0, 0, 0)),
            pl.BlockSpec((t, t), lambda bi, hi, qi: (0, 0)),
        ],
        out_specs=pl.BlockSpec((1, nh, dh, t), lambda bi, hi, qi: (bi, hi, 0, qi)),
        out_shape=jax.ShapeDtypeStruct((b, h, dh, s), BF16),
        scratch_shapes=[pltpu.VMEM((dh, t), F32)] * nh + [pltpu.VMEM((1, t), F32)] * nh,
        compiler_params=_params("parallel", "parallel", "arbitrary"),
        name="sb_attention",
    )(qT, kb, vT, tri)
    return oT.transpose(0, 3, 1, 2).reshape(b, s, SB_WIDTH)


CONV_HALO = 32


def _even_post_kernel(x_ref, att_ref, ucur_ref, uhalo_ref, dww_ref, dwb_ref, lng_ref, lnb_ref,
                      wout_ref, g1_ref, o_ref, ext_ref, *shift_refs, tm, taps):
    i = pl.program_id(1)
    ext_ref[0:CONV_HALO, :] = jnp.where(i > 0, uhalo_ref[0], 0.0)
    ext_ref[CONV_HALO:CONV_HALO + tm, :] = ucur_ref[0]
    base = CONV_HALO - (taps - 1)
    acc = None
    for m in range(8):
        offs = [o for o in range(base, base + taps) if o % 8 == m]
        if not offs:
            continue
        if m == 0:
            src = ext_ref
        else:
            src = shift_refs[m % 2]
            span = offs[-1] - m + tm
            src[0:span, :] = ext_ref[m:m + span, :]
        for o in offs:
            term = src[o - m:o - m + tm, :] * dww_ref[o - base:o - base + 1, :]
            acc = term if acc is None else acc + term
    conv = acc + dwb_ref[...]
    mu = jnp.mean(conv, axis=-1, keepdims=True)
    cen = conv - mu
    var = jnp.mean(cen * cen, axis=-1, keepdims=True)
    y = cen * lax.rsqrt(var + EPS) * lng_ref[...] + lnb_ref[...]
    c = (y * _sigmoid(y)).astype(BF16)
    cat = jnp.concatenate([att_ref[0], c], axis=-1)
    out = jnp.dot(cat, wout_ref[...], preferred_element_type=F32)
    o_ref[0] = x_ref[0] + g1_ref[0] * out


def _even_post(x, att, u, dw_w, dw_b, ln_g, ln_b, w_out, g1, tm):
    b, s, d = x.shape
    cw = u.shape[-1]
    taps = dw_w.shape[0]
    hb = tm // CONV_HALO
    return pl.pallas_call(
        functools.partial(_even_post_kernel, tm=tm, taps=taps),
        grid=(b, s // tm),
        in_specs=[
            pl.BlockSpec((1, tm, d), lambda bi, i: (bi, i, 0)),
            pl.BlockSpec((1, tm, att.shape[-1]), lambda bi, i: (bi, i, 0)),
            pl.BlockSpec((1, tm, cw), lambda bi, i: (bi, i, 0)),
            pl.BlockSpec((1, CONV_HALO, cw), lambda bi, i: (bi, jnp.maximum(i * hb - 1, 0), 0)),
            pl.BlockSpec((taps, cw), lambda bi, i: (0, 0)),
            pl.BlockSpec((1, cw), lambda bi, i: (0, 0)),
            pl.BlockSpec((1, cw), lambda bi, i: (0, 0)),
            pl.BlockSpec((1, cw), lambda bi, i: (0, 0)),
            pl.BlockSpec(w_out.shape, lambda bi, i: (0, 0)),
            pl.BlockSpec((1, 1, d), lambda bi, i: (bi, 0, 0)),
        ],
        out_specs=pl.BlockSpec((1, tm, d), lambda bi, i: (bi, i, 0)),
        out_shape=jax.ShapeDtypeStruct((b, s, d), F32),
        scratch_shapes=[pltpu.VMEM((CONV_HALO + tm, cw), F32)] * 3,
        compiler_params=_params("parallel", "parallel"),
        name="even_post",
    )(x, att, u, u, dw_w.reshape(taps, cw), dw_b.reshape(1, cw), ln_g.reshape(1, cw),
      ln_b.reshape(1, cw), w_out, g1)


def _odd_in_kernel(x_ref, g_ref, sh_ref, sc_ref, w_ref, u_ref):
    h = _norm_mod(x_ref[0], g_ref[...], sh_ref[0], sc_ref[0]).astype(BF16)
    u_ref[0] = jnp.dot(h, w_ref[...], preferred_element_type=F32)


def _odd_in(x, g, shift, scale, w_in, tm):
    b, s, d = x.shape
    return pl.pallas_call(
        _odd_in_kernel,
        grid=(b, s // tm),
        in_specs=[
            pl.BlockSpec((1, tm, d), lambda bi, i: (bi, i, 0)),
            pl.BlockSpec((1, d), lambda bi, i: (0, 0)),
            pl.BlockSpec((1, 1, d), lambda bi, i: (bi, 0, 0)),
            pl.BlockSpec((1, 1, d), lambda bi, i: (bi, 0, 0)),
            pl.BlockSpec(w_in.shape, lambda bi, i: (0, 0)),
        ],
        out_specs=pl.BlockSpec((1, tm, w_in.shape[1]), lambda bi, i: (bi, i, 0)),
        out_shape=jax.ShapeDtypeStruct((b, s, w_in.shape[1]), F32),
        compiler_params=_params("parallel", "parallel"),
        name="odd_in",
    )(x, g.reshape(1, d), shift, scale, w_in)


POOL_HALO = 16
POOL_PAD = 8


def _odd_post_kernel(x_ref, ucur_ref, uhalo_ref, wgrp_ref, scale_ref, wout_ref, g1_ref, o_ref,
                     e_ref, s1_ref, s2_ref, s4_ref, *, tm):
    i = pl.program_id(1)
    d = e_ref.shape[1]
    gw = d // len(POOL_WINDOWS)
    ext = POOL_HALO + tm
    lo = POOL_PAD
    zeros_pad = jnp.zeros((POOL_PAD, d), F32)
    e_ref[0:lo, :] = zeros_pad
    e_ref[lo:lo + POOL_HALO, :] = jnp.where(i > 0, uhalo_ref[0], 0.0)
    e_ref[lo + POOL_HALO:lo + ext, :] = ucur_ref[0]
    s1_ref[0:lo, :] = zeros_pad
    s1_ref[lo:lo + ext, :] = e_ref[lo:lo + ext, :] + e_ref[lo - 1:lo - 1 + ext, :]
    s2_ref[0:lo, :] = zeros_pad
    s2_ref[lo:lo + ext, :] = s1_ref[lo:lo + ext, :] + s1_ref[lo - 2:lo - 2 + ext, :]
    s4_ref[0:lo, :] = zeros_pad
    s4_ref[lo:lo + ext, :] = s2_ref[lo:lo + ext, :] + s2_ref[lo - 4:lo - 4 + ext, :]
    t0 = lo + POOL_HALO
    pos = i * tm + lax.broadcasted_iota(jnp.int32, (tm, 1), 0) + 1
    sums = (
        s1_ref[t0:t0 + tm, 0:gw],
        s2_ref[t0:t0 + tm, gw:2 * gw],
        s4_ref[t0:t0 + tm, 2 * gw:3 * gw],
        s4_ref[t0:t0 + tm, 3 * gw:4 * gw] + s4_ref[t0 - 8:t0 - 8 + tm, 3 * gw:4 * gw],
    )
    outs = []
    for gi, w in enumerate(POOL_WINDOWS):
        cnt = jnp.minimum(pos, w).astype(F32)
        ug = e_ref[t0:t0 + tm, gi * gw:(gi + 1) * gw]
        pooled = sums[gi] / cnt - ug
        outs.append(jnp.dot(pooled.astype(BF16), wgrp_ref[gi], preferred_element_type=F32))
    y = (jnp.concatenate(outs, axis=-1) * scale_ref[...]).astype(BF16)
    out = jnp.dot(y, wout_ref[...], preferred_element_type=F32)
    o_ref[0] = x_ref[0] + g1_ref[0] * out


def _odd_post(x, u, w_grp, ch_scale, w_out, g1, tm):
    b, s, d = x.shape
    hb = tm // POOL_HALO
    rows = POOL_PAD + POOL_HALO + tm
    return pl.pallas_call(
        functools.partial(_odd_post_kernel, tm=tm),
        grid=(b, s // tm),
        in_specs=[
            pl.BlockSpec((1, tm, d), lambda bi, i: (bi, i, 0)),
            pl.BlockSpec((1, tm, d), lambda bi, i: (bi, i, 0)),
            pl.BlockSpec((1, POOL_HALO, d), lambda bi, i: (bi, jnp.maximum(i * hb - 1, 0), 0)),
            pl.BlockSpec(w_grp.shape, lambda bi, i: (0, 0, 0)),
            pl.BlockSpec((1, d), lambda bi, i: (0, 0)),
            pl.BlockSpec(w_out.shape, lambda bi, i: (0, 0)),
            pl.BlockSpec((1, 1, d), lambda bi, i: (bi, 0, 0)),
        ],
        out_specs=pl.BlockSpec((1, tm, d), lambda bi, i: (bi, i, 0)),
        out_shape=jax.ShapeDtypeStruct((b, s, d), F32),
        scratch_shapes=[pltpu.VMEM((rows, d), F32)] * 4,
        compiler_params=_params("parallel", "parallel"),
        name="odd_post",
    )(x, u, u, w_grp, ch_scale.reshape(1, d), w_out, g1)


def _route_kernel(x_ref, g_ref, sh_ref, sc_ref, wr_ref, bias_ref, h_ref, wt_ref, cnt_ref, *, n_exp):
    h = _norm_mod(x_ref[0], g_ref[...], sh_ref[0], sc_ref[0])
    h_hi = h.astype(BF16)
    h_lo = (h - h_hi.astype(F32)).astype(BF16)
    h_ref[...] = h_hi
    nt = (((1,), (1,)), ((), ()))
    both = lax.dot_general(wr_ref[...], h_hi, nt, preferred_element_type=F32)
    logits = (both[:n_exp] + both[n_exp:]
              + lax.dot_general(wr_ref[0:n_exp, :], h_lo, nt, preferred_element_type=F32))
    scores = _sigmoid(logits)
    biased = scores + bias_ref[...]
    per = n_exp // N_GROUPS
    tm = scores.shape[1]
    sub = lax.broadcasted_iota(jnp.int32, (per, tm), 0)
    slabs = [biased[g * per:(g + 1) * per, :] for g in range(N_GROUPS)]
    gs = []
    for sl in slabs:
        m1 = jnp.max(sl, axis=0, keepdims=True)
        first = jnp.min(jnp.where(sl == m1, sub, per), axis=0, keepdims=True)
        m2 = jnp.max(jnp.where(sub == first, -jnp.inf, sl), axis=0, keepdims=True)
        gs.append(m1 + m2)
    masked = []
    for g in range(N_GROUPS):
        rank = jnp.zeros((1, tm), jnp.int32)
        for g2 in range(N_GROUPS):
            if g2 < g:
                rank = rank + (gs[g2] >= gs[g]).astype(jnp.int32)
            elif g2 > g:
                rank = rank + (gs[g2] > gs[g]).astype(jnp.int32)
        masked.append(jnp.where(rank < TOPK_GROUPS, slabs[g], -jnp.inf))
    ranks = [jnp.zeros((per, tm), jnp.int32) for _ in range(N_GROUPS)]
    for g2 in range(N_GROUPS):
        for i2 in range(per):
            r = masked[g2][i2:i2 + 1, :]
            for g in range(N_GROUPS):
                if g2 < g:
                    beats = r >= masked[g]
                elif g2 > g:
                    beats = r > masked[g]
                else:
                    beats = (r > masked[g]) | ((r == masked[g]) & (sub > i2))
                ranks[g] = ranks[g] + beats.astype(jnp.int32)
    picked = [jnp.where(ranks[g] < TOP_K, scores[g * per:(g + 1) * per, :], 0.0)
              for g in range(N_GROUPS)]
    tot = picked[0]
    for g in range(1, N_GROUPS):
        tot = tot + picked[g]
    denom = jnp.sum(tot, axis=0, keepdims=True)
    for g in range(N_GROUPS):
        wt_ref[g * per:(g + 1) * per, :] = picked[g] / denom * ROUTED_SCALE
    ones = jnp.ones((MOE_SUB, LANES), BF16)
    for k in range(tm // MOE_SUB):
        sel = jnp.where(wt_ref[:, k * MOE_SUB:(k + 1) * MOE_SUB] > 0.0, 1.0, 0.0).astype(BF16)
        cnt_ref[:, k * LANES:(k + 1) * LANES] = jnp.dot(sel, ones, preferred_element_type=F32)


def _route(x, g, shift, scale, wr2, bias, tm):
    b, s, d = x.shape
    n = b * s
    n_exp = bias.shape[0]
    per_b = s // tm
    cw = tm // MOE_SUB * LANES
    return pl.pallas_call(
        functools.partial(_route_kernel, n_exp=n_exp),
        grid=(b, per_b),
        in_specs=[
            pl.BlockSpec((1, tm, d), lambda bi, i: (bi, i, 0)),
            pl.BlockSpec((1, d), lambda bi, i: (0, 0)),
            pl.BlockSpec((1, 1, d), lambda bi, i: (bi, 0, 0)),
            pl.BlockSpec((1, 1, d), lambda bi, i: (bi, 0, 0)),
            pl.BlockSpec(wr2.shape, lambda bi, i: (0, 0)),
            pl.BlockSpec((n_exp, 1), lambda bi, i: (0, 0)),
        ],
        out_specs=[
            pl.BlockSpec((tm, d), lambda bi, i: (bi * per_b + i, 0)),
            pl.BlockSpec((n_exp, tm), lambda bi, i: (0, bi * per_b + i)),
            pl.BlockSpec((n_exp, cw), lambda bi, i: (0, bi * per_b + i)),
        ],
        out_shape=[
            jax.ShapeDtypeStruct((n, d), BF16),
            jax.ShapeDtypeStruct((n_exp, n), F32),
            jax.ShapeDtypeStruct((n_exp, n // MOE_SUB * LANES), F32),
        ],
        compiler_params=_params("parallel", "parallel"),
        name="moe_route",
    )(x, g.reshape(1, d), shift, scale, wr2, bias.reshape(n_exp, 1))


MOE_SUB = 256
MOE_TILE = 16
MOE_ROWBLK = 1024
MOE_CHUNK = 512
TILES_PER_CHUNK = MOE_CHUNK // MOE_TILE
TILES_PER_BLK = MOE_ROWBLK // MOE_TILE
TAB_W = 256


def _max_tiles(n_exp):
    worst = (TOP_K * MOE_SUB + n_exp * (MOE_TILE - 1)) // MOE_TILE
    return -(-worst // TILES_PER_CHUNK) * TILES_PER_CHUNK


def _dispatch_tables(cnt, n_sub, n_exp):
    i32 = jnp.int32
    nt_max = _max_tiles(n_exp)
    blk = MOE_ROWBLK // MOE_TILE
    n = cnt[:, ::LANES].T.astype(i32)
    np16 = (n + (MOE_TILE - 1)) // MOE_TILE
    q_end = jnp.cumsum(np16, axis=1)
    q_off = q_end - np16
    n_tiles = q_end[:, -1]
    g_off = jnp.cumsum(np16, axis=0) - np16
    tot = jnp.sum(np16, axis=0)
    tot_r = (tot + (blk - 1)) // blk * blk
    bend = jnp.cumsum(tot_r)
    base = bend - tot_r
    j = jnp.arange(nt_max, dtype=i32)[None, :, None]
    owns = (j >= q_off[:, None, :]) & (j < q_end[:, None, :])
    shift = (base[None, :] + g_off - q_off)[:, None, :]
    dest = jnp.sum(jnp.where(owns, shift + j, 0), axis=-1)
    tab = jnp.concatenate(
        [n_tiles[:, None], dest, jnp.zeros((n_sub, TAB_W - 1 - nt_max), i32)], axis=1)
    n_blocks = (TOP_K * n_sub * MOE_SUB + n_sub * n_exp * (MOE_TILE - 1)
                + n_exp * (MOE_ROWBLK - 1)) // MOE_ROWBLK + 1
    n_used = (bend[-1] // blk).astype(i32).reshape(1)
    gaps = jnp.stack([base + tot, tot_r - tot, jnp.broadcast_to(n_used, (n_exp,))]).astype(i32)
    b_idx = jnp.minimum(jnp.arange(n_blocks, dtype=i32), n_used[0] - 1)
    block_e = jnp.minimum(jnp.sum(b_idx[:, None] * blk >= bend[None, :], axis=1), n_exp - 1)
    row_lo = (q_off * MOE_TILE).astype(F32)
    row_hi = (q_end * MOE_TILE).astype(F32)
    pad_l = ((0, 0), (0, LANES - n_exp))
    lane_tabs = (jnp.pad(row_lo, pad_l).reshape(n_sub, 1, LANES),
                 jnp.pad(row_hi, pad_l).reshape(n_sub, 1, LANES))
    pad_s = ((0, 0), (0, LANES - n_exp), (0, 0))
    sub_tabs = (jnp.pad(jnp.broadcast_to(row_lo[:, :, None], (n_sub, n_exp, LANES)), pad_s),
                jnp.pad(jnp.broadcast_to(row_hi[:, :, None], (n_sub, n_exp, LANES)), pad_s))
    q_row = jnp.pad(q_off.astype(F32), pad_l)
    q_row8 = jnp.broadcast_to(q_row[:, None, :], (n_sub, 8, LANES)).astype(BF16)
    q_col = jnp.broadcast_to(q_off.astype(F32)[:, :, None], (n_sub, n_exp, LANES)).astype(BF16)
    return dict(tab=tab.reshape(n_sub, 1, TAB_W), gaps=gaps, n_blocks=n_blocks, n_used=n_used,
                block_e=block_e.astype(i32), lane_tabs=lane_tabs, sub_tabs=sub_tabs,
                q_row8=q_row8, q_col=q_col)


def _tile_rows(tile):
    if isinstance(tile, int):
        return pl.ds(tile * MOE_TILE, MOE_TILE)
    return pl.ds(pl.multiple_of(tile * MOE_TILE, MOE_TILE), MOE_TILE)


def _tile_copy(src_ref, src_tile, dst_ref, dst_tile, sem):
    return pltpu.make_async_copy(src_ref.at[_tile_rows(src_tile)], dst_ref.at[_tile_rows(dst_tile)],
                                 sem)


def _dispatch_kernel(tab_ref, gap_ref, h_ref, wt_ref, lo_ref, hi_ref, qcol_ref, su_ref, xg_ref,
                     stage_ref, zero_ref, sem, *, n_exp, d, n_blocks):
    s = pl.program_id(0)
    spare_tile0 = n_blocks * TILES_PER_BLK
    n_blk_total = xg_ref.shape[0] // MOE_ROWBLK
    n_tiles = tab_ref[0, 0, 0]
    n_chunks = (n_tiles + TILES_PER_CHUNK - 1) // TILES_PER_CHUNK
    wt = wt_ref[...]
    sel = wt > 0.0
    pos = jnp.dot(jnp.where(sel, 1.0, 0.0).astype(BF16), su_ref[...], preferred_element_type=F32)
    posm = jnp.where(sel, pos, -1.0).astype(BF16)
    wt_hi = wt.astype(BF16)
    wt_lo = (wt - wt_hi.astype(F32)).astype(BF16)
    top = jnp.concatenate([posm, qcol_ref[0], wt_hi, wt_lo], axis=1)
    rhs = jnp.concatenate([top, jnp.zeros((LANES - n_exp, top.shape[1]), BF16)], axis=0)
    h = h_ref[...]
    row_lo = lo_ref[0]
    row_hi = hi_ref[0]
    lane = lax.broadcasted_iota(jnp.int32, (MOE_CHUNK, LANES), 1)
    sub = MOE_SUB

    def chunk(c, carry):
        r = (c * MOE_CHUNK + lax.broadcasted_iota(jnp.int32, (MOE_CHUNK, LANES), 0)).astype(F32)
        owner = jnp.where((r >= row_lo) & (r < row_hi), 1.0, 0.0).astype(BF16)
        g = jnp.dot(owner, rhs, preferred_element_type=F32)
        j = r - MOE_TILE * g[:, sub:sub + LANES]
        hit = g[:, 0:sub] == jnp.concatenate([j] * (sub // LANES), axis=1)
        rows = jnp.dot(jnp.where(hit, 1.0, 0.0).astype(BF16), h, preferred_element_type=F32)
        wg = g[:, sub + LANES:2 * sub + LANES] + g[:, 2 * sub + LANES:]
        w = jnp.sum(jnp.where(hit, wg, 0.0), axis=1, keepdims=True)
        w_hi = w.astype(BF16).astype(F32)
        w_lo = (w - w_hi).astype(BF16).astype(F32)
        r0 = pl.multiple_of(c * MOE_CHUNK, MOE_CHUNK)
        stage_ref[pl.ds(r0, MOE_CHUNK), 0:d] = rows.astype(BF16)
        stage_ref[pl.ds(r0, MOE_CHUNK), d:] = jnp.where(lane < LANES // 2, w_hi, w_lo).astype(BF16)
        for jt in range(TILES_PER_CHUNK):
            t = c * TILES_PER_CHUNK + jt
            dst = jnp.where(t < n_tiles, tab_ref[0, 0, 1 + t], spare_tile0 + t)
            _tile_copy(stage_ref, t, xg_ref, dst, sem.at[0]).start()
        return carry

    lax.fori_loop(0, n_chunks, chunk, 0)

    def drain(c, carry):
        for jt in range(TILES_PER_CHUNK):
            _tile_copy(stage_ref, 0, xg_ref, 0, sem.at[0]).wait()
        return carry

    lax.fori_loop(0, n_chunks, drain, 0)

    @pl.when(s == pl.num_programs(0) - 1)
    def _():
        zero_ref[...] = jnp.zeros_like(zero_ref)

        def gap_copy(dst_tile):
            return _tile_copy(zero_ref, 0, xg_ref, dst_tile, sem.at[1])

        def blk_copy(blk):
            r0 = blk * MOE_ROWBLK
            if not isinstance(blk, int):
                r0 = pl.multiple_of(r0, MOE_ROWBLK)
            return pltpu.make_async_copy(zero_ref, xg_ref.at[pl.ds(r0, MOE_ROWBLK)], sem.at[2])

        def per_expert(e, carry):
            def one(k, c2):
                gap_copy(gap_ref[0, e] + k).start()
                return c2
            return lax.fori_loop(0, gap_ref[1, e], one, carry)

        lax.fori_loop(0, n_exp, per_expert, 0)
        n_used = gap_ref[2, 0]

        def fill(blk, carry):
            blk_copy(blk).start()
            return carry

        lax.fori_loop(n_used, n_blk_total, fill, 0)

        def per_expert_wait(e, carry):
            def one(k, c2):
                gap_copy(0).wait()
                return c2
            return lax.fori_loop(0, gap_ref[1, e], one, carry)

        lax.fori_loop(0, n_exp, per_expert_wait, 0)

        def fill_wait(blk, carry):
            blk_copy(0).wait()
            return carry

        lax.fori_loop(n_used, n_blk_total, fill_wait, 0)


def _dispatch(h, wt, tabs, n_exp):
    n, d = h.shape
    n_sub = n // MOE_SUB
    xw = d + LANES
    stage_rows = _max_tiles(n_exp) * MOE_TILE
    su = (jnp.arange(MOE_SUB)[:, None] < jnp.arange(MOE_SUB)[None, :]).astype(BF16)
    smem = pltpu.SMEM
    total_rows = tabs["n_blocks"] * MOE_ROWBLK + -(-stage_rows // MOE_ROWBLK) * MOE_ROWBLK
    return pl.pallas_call(
        functools.partial(_dispatch_kernel, n_exp=n_exp, d=d, n_blocks=tabs["n_blocks"]),
        grid=(n_sub,),
        in_specs=[
            pl.BlockSpec((1, 1, TAB_W), lambda s: (s, 0, 0), memory_space=smem),
            pl.BlockSpec(memory_space=smem),
            pl.BlockSpec((MOE_SUB, d), lambda s: (s, 0)),
            pl.BlockSpec((n_exp, MOE_SUB), lambda s: (0, s)),
            pl.BlockSpec((1, 1, LANES), lambda s: (s, 0, 0)),
            pl.BlockSpec((1, 1, LANES), lambda s: (s, 0, 0)),
            pl.BlockSpec((1, n_exp, LANES), lambda s: (s, 0, 0)),
            pl.BlockSpec((MOE_SUB, MOE_SUB), lambda s: (0, 0)),
        ],
        out_specs=pl.BlockSpec(memory_space=pl.ANY),
        out_shape=jax.ShapeDtypeStruct((total_rows, xw), BF16),
        scratch_shapes=[
            pltpu.VMEM((stage_rows, xw), BF16),
            pltpu.VMEM((MOE_ROWBLK, xw), BF16),
            pltpu.SemaphoreType.DMA((3,)),
        ],
        compiler_params=_params("arbitrary"),
        name="moe_dispatch",
    )(tabs["tab"], tabs["gaps"], h, wt, tabs["lane_tabs"][0], tabs["lane_tabs"][1], tabs["q_col"],
      su)


def _expert_kernel(be_ref, nu_ref, x_ref, w13_ref, w2_ref, y_ref, *, d, f):
    @pl.when(pl.program_id(0) >= nu_ref[0])
    def _():
        y_ref[...] = jnp.zeros_like(y_ref)

    @pl.when(pl.program_id(0) < nu_ref[0])
    def _():
        x = x_ref[...]
        half = LANES // 2
        w = x[:, d:d + 1].astype(F32) + x[:, d + half:d + half + 1].astype(F32)
        ab = jnp.dot(x[:, 0:d], w13_ref[0], preferred_element_type=F32)
        a = ab[:, 0:f]
        gated = (a * _sigmoid(a)) * ab[:, f:] * w
        y_ref[...] = jnp.dot(gated.astype(BF16), w2_ref[0],
                             preferred_element_type=F32).astype(y_ref.dtype)


def _experts(xg, w13, w2, tabs):
    rows, xw = xg.shape
    n_exp, d, f2 = w13.shape
    f = f2 // 2

    def blk(b, be, nu):
        return jnp.minimum(b, nu[0] - 1)

    return pl.pallas_call(
        functools.partial(_expert_kernel, d=d, f=f),
        grid_spec=pltpu.PrefetchScalarGridSpec(
            num_scalar_prefetch=2,
            grid=(tabs["n_blocks"],),
            in_specs=[
                pl.BlockSpec((MOE_ROWBLK, xw), lambda b, be, nu: (blk(b, be, nu), 0)),
                pl.BlockSpec((1, d, f2), lambda b, be, nu: (be[b], 0, 0)),
                pl.BlockSpec((1, f, d), lambda b, be, nu: (be[b], 0, 0)),
            ],
            out_specs=pl.BlockSpec((MOE_ROWBLK, d), lambda b, be, nu: (b, 0)),
        ),
        out_shape=jax.ShapeDtypeStruct((tabs["n_blocks"] * MOE_ROWBLK, d), BF16),
        compiler_params=_params("arbitrary"),
        name="moe_experts",
    )(tabs["block_e"], tabs["n_used"], xg, w13, w2)


def _combine_kernel(tab_ref, y_ref, wtT_ref, lo_ref, hi_ref, q8_ref, sl_ref, h_ref, s13_ref,
                    s2_ref, x_ref, g2_ref, o_ref, ybuf_ref, acc_ref, sem, *, f):
    n_tiles = tab_ref[0, 0, 0]
    n_chunks = (n_tiles + TILES_PER_CHUNK - 1) // TILES_PER_CHUNK
    d = ybuf_ref.shape[1]

    def fetch(c, carry):
        for jt in range(TILES_PER_CHUNK):
            t = c * TILES_PER_CHUNK + jt
            _tile_copy(y_ref, tab_ref[0, 0, 1 + t], ybuf_ref, t, sem.at[0]).start()
        return carry

    lax.fori_loop(0, n_chunks, fetch, 0)

    ab = jnp.dot(h_ref[...], s13_ref[...], preferred_element_type=F32)
    a = ab[:, 0:f]
    acc_ref[...] = jnp.dot(((a * _sigmoid(a)) * ab[:, f:]).astype(BF16), s2_ref[...],
                           preferred_element_type=F32)

    sel = wtT_ref[...] > 0.0
    pos = jnp.dot(sl_ref[...], jnp.where(sel, 1.0, 0.0).astype(BF16), preferred_element_type=F32)
    posm = jnp.where(sel, pos, -1.0).astype(BF16)
    row_lo = jnp.concatenate([lo_ref[0]] * (MOE_CHUNK // LANES), axis=1)
    row_hi = jnp.concatenate([hi_ref[0]] * (MOE_CHUNK // LANES), axis=1)
    q8 = q8_ref[0]

    def drain(c, carry):
        for jt in range(TILES_PER_CHUNK):
            _tile_copy(y_ref, 0, ybuf_ref, 0, sem.at[0]).wait()
        return carry

    lax.fori_loop(0, n_chunks, drain, 0)

    def chunk(c, carry):
        r = (c * MOE_CHUNK + lax.broadcasted_iota(jnp.int32, (LANES, MOE_CHUNK), 1)).astype(F32)
        owner = jnp.where((r >= row_lo) & (r < row_hi), 1.0, 0.0).astype(BF16)
        g = jnp.dot(posm, owner, preferred_element_type=F32)
        q = jnp.dot(q8, owner, preferred_element_type=F32)
        j = r[0:1, :] - MOE_TILE * q[0:1, :]
        hit = jnp.where(g == j, 1.0, 0.0).astype(BF16)
        r0 = pl.multiple_of(c * MOE_CHUNK, MOE_CHUNK)
        acc_ref[...] += jnp.dot(hit, ybuf_ref[pl.ds(r0, MOE_CHUNK), :], preferred_element_type=F32)
        return carry

    lax.fori_loop(0, n_chunks, chunk, 0)
    o_ref[0] = x_ref[0] + g2_ref[0] * acc_ref[...]


def _combine(y, wt, tabs, h, s13, s2, x, g2):
    b, s, d = x.shape
    n = b * s
    n_exp = wt.shape[0]
    n_sub = n // MOE_SUB
    per_b = s // MOE_SUB
    f = s2.shape[0]
    wtT = jnp.pad(wt.T, ((0, 0), (0, LANES - n_exp)))
    sl = (jnp.arange(MOE_SUB)[None, :] < jnp.arange(MOE_SUB)[:, None]).astype(BF16)
    ybuf_rows = _max_tiles(n_exp) * MOE_TILE
    return pl.pallas_call(
        functools.partial(_combine_kernel, f=f),
        grid=(n_sub,),
        in_specs=[
            pl.BlockSpec((1, 1, TAB_W), lambda i: (i, 0, 0), memory_space=pltpu.SMEM),
            pl.BlockSpec(memory_space=pl.ANY),
            pl.BlockSpec((MOE_SUB, LANES), lambda i: (i, 0)),
            pl.BlockSpec((1, LANES, LANES), lambda i: (i, 0, 0)),
            pl.BlockSpec((1, LANES, LANES), lambda i: (i, 0, 0)),
            pl.BlockSpec((1, 8, LANES), lambda i: (i, 0, 0)),
            pl.BlockSpec((MOE_SUB, MOE_SUB), lambda i: (0, 0)),
            pl.BlockSpec((MOE_SUB, d), lambda i: (i, 0)),
            pl.BlockSpec(s13.shape, lambda i: (0, 0)),
            pl.BlockSpec(s2.shape, lambda i: (0, 0)),
            pl.BlockSpec((1, MOE_SUB, d), lambda i: (i // per_b, i % per_b, 0)),
            pl.BlockSpec((1, 1, d), lambda i: (i // per_b, 0, 0)),
        ],
        out_specs=pl.BlockSpec((1, MOE_SUB, d), lambda i: (i // per_b, i % per_b, 0)),
        out_shape=jax.ShapeDtypeStruct((b, s, d), F32),
        scratch_shapes=[
            pltpu.VMEM((ybuf_rows, d), BF16),
            pltpu.VMEM((MOE_SUB, d), F32),
            pltpu.SemaphoreType.DMA((1,)),
        ],
        compiler_params=_params("arbitrary"),
        name="moe_combine",
    )(tabs["tab"], y, wtT, tabs["sub_tabs"][0], tabs["sub_tabs"][1], tabs["q_row8"], sl, h, s13,
      s2, x, g2)


def _final_kernel(x_ref, g_ref, o_ref):
    x = x_ref[0]
    ms = jnp.mean(x * x, axis=-1, keepdims=True)
    o_ref[0] = x * lax.rsqrt(ms + EPS) * g_ref[...]


def _final_norm(x, g, tm):
    b, s, d = x.shape
    return pl.pallas_call(
        _final_kernel,
        grid=(b, s // tm),
        in_specs=[
            pl.BlockSpec((1, tm, d), lambda bi, i: (bi, i, 0)),
            pl.BlockSpec((1, d), lambda bi, i: (0, 0)),
        ],
        out_specs=pl.BlockSpec((1, tm, d), lambda bi, i: (bi, i, 0)),
        out_shape=jax.ShapeDtypeStruct((b, s, d), F32),
        compiler_params=_params("parallel", "parallel"),
        name="final_norm",
    )(x, g.reshape(1, d))


def _tile(s, want):
    t = min(s, want)
    assert s % t == 0, (s, t)
    return t


def kernel(x, c, ada_w, ada_b, norm1_g, norm2_g, ev_w_in, ev_dw_w, ev_dw_b, ev_ln_g, ev_ln_b,
           ev_w_out, od_w_in, od_w_grp, od_scale, od_w_out, moe_w_router, moe_bias, moe_w1, moe_w3,
           moe_w2, sh_w1, sh_w3, sh_w2, final_g):
    b, s, d = x.shape
    depth = ada_w.shape[0]
    tm = _tile(s, 512)
    n_exp = moe_bias.shape[1]
    n_sub = b * s // MOE_SUB
    assert s % MOE_SUB == 0 and _max_tiles(n_exp) < TAB_W, (s, n_exp)

    mod = _ada(c, ada_w, ada_b)
    for l in range(depth):
        sh1, sc1, g1, sh2, sc2, g2 = [m.reshape(b, 1, d) for m in jnp.split(mod[l], 6, axis=-1)]
        i = l // 2
        if l % 2 == 0:
            qkv, u = _even_in(x, norm1_g[l], sh1, sc1, ev_w_in[i].astype(BF16), tm)
            att = _sb_attention(qkv, _tile(s, 256), SB_HEADS)
            x = _even_post(x, att, u, ev_dw_w[i], ev_dw_b[i], ev_ln_g[i], ev_ln_b[i],
                           ev_w_out[i].astype(BF16), g1, tm)
        else:
            u = _odd_in(x, norm1_g[l], sh1, sc1, od_w_in[i].astype(BF16), tm)
            x = _odd_post(x, u, od_w_grp[i].astype(BF16), od_scale[i], od_w_out[i].astype(BF16),
                          g1, _tile(s, 256))
        wr = moe_w_router[l].T
        wr_hi = wr.astype(BF16)
        wr_lo = (wr - wr_hi.astype(F32)).astype(BF16)
        h, wt, cnt = _route(x, norm2_g[l], sh2, sc2, jnp.concatenate([wr_hi, wr_lo], axis=0),
                            moe_bias[l], tm)
        tabs = _dispatch_tables(cnt, n_sub, n_exp)
        xg = _dispatch(h, wt, tabs, n_exp)
        w13 = jnp.concatenate([moe_w1[l], moe_w3[l]], axis=-1).astype(BF16)
        y = _experts(xg, w13, moe_w2[l].astype(BF16), tabs)
        s13 = jnp.concatenate([sh_w1[l], sh_w3[l]], axis=-1).astype(BF16)
        x = _combine(y, wt, tabs, h, s13, sh_w2[l].astype(BF16), x, g2)
    return _final_norm(x, final_g, tm)
```
